```python
import jax, jax.numpy as jnp
from jax import lax
import numpy as np

D_MODEL = 1024
BATCH = 32
SEQ = 2048
DEPTH = 4

CTX_LEN = 256
GRID_W = 64
ROPE_BASE = 10000.0
EPS = 1e-6
Q_BLOCK = 128

MLA_HEADS = 8
MLA_Q_RANK = 256
MLA_KV_RANK = 128
MLA_NOPE = 64
MLA_ROPE = 32
MLA_V = 64
MLA_QK = MLA_NOPE + MLA_ROPE

CONV_CH = 512
CONV_W = 31

GQA_HEADS = 8
GQA_KV_HEADS = 2
GQA_HD = 64

N_BRANCH = 3
IN_SIZES = (MLA_Q_RANK, MLA_KV_RANK, MLA_ROPE, 2 * CONV_CH, GQA_HEADS * GQA_HD, GQA_KV_HEADS * GQA_HD, GQA_KV_HEADS * GQA_HD, N_BRANCH * D_MODEL)
D_IN = MLA_Q_RANK + MLA_KV_RANK + MLA_ROPE + 2 * CONV_CH + GQA_HEADS * GQA_HD + 2 * GQA_KV_HEADS * GQA_HD + N_BRANCH * D_MODEL

N_EXPERTS = 16
N_GROUPS = 4
EXPERTS_PER_GROUP = N_EXPERTS // N_GROUPS
TOP_K = 2
D_EXPERT = 512

kernel_name = "hybrid_mla_conformer_gqa_moe_dit"


def rms_norm(x, g):
    xf = x.astype(jnp.float32)
    y = xf * lax.rsqrt(jnp.mean(xf * xf, axis=-1, keepdims=True) + EPS)
    return (y * g.astype(jnp.float32)).astype(x.dtype)


def layer_norm(x, g, b):
    xf = x.astype(jnp.float32)
    mu = jnp.mean(xf, axis=-1, keepdims=True)
    var = jnp.mean(jnp.square(xf - mu), axis=-1, keepdims=True)
    y = (xf - mu) * lax.rsqrt(var + EPS)
    return (y * g.astype(jnp.float32) + b.astype(jnp.float32)).astype(x.dtype)


def split_cols(p):
    parts, start = [], 0
    for size in IN_SIZES:
        parts.append(p[..., start:start + size])
        start += size
    return parts


def rope_angles(pos, dim):
    inv = ROPE_BASE ** (-jnp.arange(0, dim, 2, dtype=jnp.float32) / dim)
    return pos.astype(jnp.float32)[:, None] * inv[None, :]


def rotate_half(x, ang):
    cos = jnp.cos(ang)[:, None, :].astype(x.dtype)
    sin = jnp.sin(ang)[:, None, :].astype(x.dtype)
    x1, x2 = jnp.split(x, 2, axis=-1)
    return jnp.concatenate([x1 * cos - x2 * sin, x2 * cos + x1 * sin], axis=-1)


def axial_rope(x, row, col):
    r = x.shape[-1]
    xr, xc = jnp.split(x, 2, axis=-1)
    return jnp.concatenate([rotate_half(xr, rope_angles(row, r // 2)), rotate_half(xc, rope_angles(col, r // 2))], axis=-1)


def block_attention(q, k, v, scale):
    b, nq, hk, g, d = q.shape
    nb = nq // Q_BLOCK
    qb = jnp.moveaxis(q.reshape(b, nb, Q_BLOCK, hk, g, d), 1, 0)

    def one_block(qi):
        s = jnp.einsum('bqhgd,bkhd->bhgqk', qi, k, preferred_element_type=jnp.float32) * scale
        p = jax.nn.softmax(s, axis=-1).astype(v.dtype)
        return jnp.einsum('bhgqk,bkhe->bqhge', p, v)

    out = lax.map(one_block, qb)
    return jnp.moveaxis(out, 0, 1).reshape(b, nq, hk, g, v.shape[-1])


def depthwise_conv(u, w, bias):
    out = lax.conv_general_dilated(u, w[:, None, :], window_strides=(1,), padding=[(CONV_W // 2, CONV_W // 2)],
                                   dimension_numbers=('NWC', 'WIO', 'NWC'), feature_group_count=u.shape[-1])
    return out + bias


def prepare(h, l, P, rope):
    b, n, _ = h.shape
    cq, ckv, kpe, conv_in, gq, gk, gv, gate_logits = split_cols(h @ P['w_in'][l])
    q = (rms_norm(cq, P['mla_q_norm'][l]) @ P['mla_w_uq'][l]).reshape(b, n, MLA_HEADS, MLA_QK)
    kv = (rms_norm(ckv, P['mla_kv_norm'][l]) @ P['mla_w_ukv'][l]).reshape(b, n, MLA_HEADS, MLA_NOPE + MLA_V)
    qg, kg = P['mla_q_gain'][l], P['mla_k_gain'][l]
    q_nope = rms_norm(q[..., :MLA_NOPE], qg[:MLA_NOPE])
    q_pe = rms_norm(q[..., MLA_NOPE:], qg[MLA_NOPE:])
    k_nope = rms_norm(kv[..., :MLA_NOPE], kg[:MLA_NOPE])
    v_mla = kv[..., MLA_NOPE:]
    k_pe = rms_norm(kpe, kg[MLA_NOPE:])[:, :, None, :]
    q_g = rms_norm(gq.reshape(b, n, GQA_HEADS, GQA_HD), P['gqa_q_gain'][l])
    k_g = rms_norm(gk.reshape(b, n, GQA_KV_HEADS, GQA_HD), P['gqa_k_gain'][l])
    v_g = gv.reshape(b, n, GQA_KV_HEADS, GQA_HD)
    if rope is not None:
        row, col = rope
        q_pe = axial_rope(q_pe, row, col)
        k_pe = axial_rope(k_pe, row, col)
        q_g = axial_rope(q_g, row, col)
        k_g = axial_rope(k_g, row, col)
    q_mla = jnp.concatenate([q_nope, q_pe], axis=-1)
    k_mla = jnp.concatenate([k_nope, jnp.broadcast_to(k_pe, (b, n, MLA_HEADS, MLA_ROPE))], axis=-1)
    return q_mla, k_mla, v_mla, q_g, k_g, v_g, conv_in, gate_logits


def conformer_conv(u, l, P):
    a, gt = jnp.split(u, 2, axis=-1)
    y = a * jax.nn.sigmoid(gt)
    y = depthwise_conv(y, P['conv_w'][l], P['conv_b'][l])
    y = jax.nn.silu(layer_norm(y, P['conv_ln_g'][l], P['conv_ln_b'][l]))
    return y @ P['conv_w_o'][l]


def mix(q_mla, q_g, conv_in, gate_logits, k_mla, v_mla, k_g, v_g, l, P):
    b, n = conv_in.shape[:2]
    o_mla = block_attention(q_mla[:, :, :, None, :], k_mla, v_mla, MLA_QK ** -0.5)
    o_mla = o_mla.reshape(b, n, MLA_HEADS * MLA_V) @ P['mla_w_o'][l]
    qg = q_g.reshape(b, n, GQA_KV_HEADS, GQA_HEADS // GQA_KV_HEADS, GQA_HD)
    o_gqa = block_attention(qg, k_g, v_g, GQA_HD ** -0.5).reshape(b, n, GQA_HEADS * GQA_HD) @ P['gqa_w_o'][l]
    o_conv = conformer_conv(conv_in, l, P)
    g = jax.nn.sigmoid(gate_logits).reshape(b, n, N_BRANCH, D_MODEL)
    merged = g[..., 0, :] * o_mla + g[..., 1, :] * o_conv + g[..., 2, :] * o_gqa
    return merged @ P['w_out'][l]


def moe(h, l, P):
    b, n, d = h.shape
    t = h.reshape(-1, d)
    s = jax.nn.sigmoid((t @ P['router_w']).astype(jnp.float32))
    sel = s + P['router_bias'].astype(jnp.float32)
    gscore = lax.top_k(sel.reshape(-1, N_GROUPS, EXPERTS_PER_GROUP), TOP_K)[0].sum(-1)
    gbest = jnp.argmax(gscore, axis=-1)
    in_group = (jnp.arange(N_EXPERTS) // EXPERTS_PER_GROUP)[None, :] == gbest[:, None]
    _, idx = lax.top_k(jnp.where(in_group, sel, -jnp.inf), TOP_K)
    w = jnp.take_along_axis(s, idx, axis=-1)
    w = w / jnp.sum(w, axis=-1, keepdims=True)
    combine = jnp.sum(jax.nn.one_hot(idx, N_EXPERTS, dtype=jnp.float32) * w[..., None], axis=1).astype(t.dtype)
    wg, wu, wd = P['moe_w_gate'][l], P['moe_w_up'][l], P['moe_w_down'][l]
    out = jnp.zeros_like(t)
    for e in range(N_EXPERTS):
        he = jax.nn.silu(t @ wg[e]) * (t @ wu[e])
        out = out + combine[:, e:e + 1] * (he @ wd[e])
    return out.reshape(b, n, d)


def setup_inputs(seed: int = 0) -> dict:
    key = jax.random.key(seed)
    ks = jax.random.split(key, 32)
    cnt = [0]

    def nrm(shape, scale):
        k = ks[cnt[0]]
        cnt[0] += 1
        return scale * jax.random.normal(k, shape, jnp.float32)

    def gain(shape):
        return 1.0 + nrm(shape, 0.02)

    L, D, E, F = DEPTH, D_MODEL, N_EXPERTS, D_EXPERT
    return {
        'x': nrm((BATCH, SEQ, D), 1.0),
        'c': nrm((BATCH, D), 1.0),
        'ctx': nrm((BATCH, CTX_LEN, D), 1.0),
        'c_ctx': nrm((D,), 1.0),
        'w_mod': nrm((L, D, 6 * D), 0.5 * D ** -0.5),
        'b_mod': nrm((L, 6 * D), 0.02),
        'norm1_g': gain((L, D)),
        'norm2_g': gain((L, D)),
        'w_in': nrm((L, D, D_IN), D ** -0.5),
        'mla_q_norm': gain((L, MLA_Q_RANK)),
        'mla_kv_norm': gain((L, MLA_KV_RANK)),
        'mla_w_uq': nrm((L, MLA_Q_RANK, MLA_HEADS * MLA_QK), MLA_Q_RANK ** -0.5),
        'mla_w_ukv': nrm((L, MLA_KV_RANK, MLA_HEADS * (MLA_NOPE + MLA_V)), MLA_KV_RANK ** -0.5),
        'mla_q_gain': gain((L, MLA_QK)),
        'mla_k_gain': gain((L, MLA_QK)),
        'mla_w_o': nrm((L, MLA_HEADS * MLA_V, D), (MLA_HEADS * MLA_V) ** -0.5),
        'conv_w': nrm((L, CONV_W, CONV_CH), CONV_W ** -0.5),
        'conv_b': nrm((L, CONV_CH), 0.02),
        'conv_ln_g': gain((L, CONV_CH)),
        'conv_ln_b': nrm((L, CONV_CH), 0.02),
        'conv_w_o': nrm((L, CONV_CH, D), CONV_CH ** -0.5),
        'gqa_q_gain': gain((L, GQA_HD)),
        'gqa_k_gain': gain((L, GQA_HD)),
        'gqa_w_o': nrm((L, GQA_HEADS * GQA_HD, D), (GQA_HEADS * GQA_HD) ** -0.5),
        'w_out': nrm((L, D, D), D ** -0.5),
        'router_w': nrm((D, E), D ** -0.5),
        'router_bias': nrm((E,), 0.01),
        'moe_w_gate': nrm((L, E, D, F), D ** -0.5),
        'moe_w_up': nrm((L, E, D, F), D ** -0.5),
        'moe_w_down': nrm((L, E, F, D), F ** -0.5),
    }


def reference(x, c, ctx, c_ctx, w_mod, b_mod, norm1_g, norm2_g, w_in, mla_q_norm, mla_kv_norm, mla_w_uq, mla_w_ukv,
              mla_q_gain, mla_k_gain, mla_w_o, conv_w, conv_b, conv_ln_g, conv_ln_b, conv_w_o, gqa_q_gain, gqa_k_gain,
              gqa_w_o, w_out, router_w, router_bias, moe_w_gate, moe_w_up, moe_w_down):
    P = dict(w_in=w_in, mla_q_norm=mla_q_norm, mla_kv_norm=mla_kv_norm, mla_w_uq=mla_w_uq, mla_w_ukv=mla_w_ukv,
             mla_q_gain=mla_q_gain, mla_k_gain=mla_k_gain, mla_w_o=mla_w_o, conv_w=conv_w, conv_b=conv_b,
             conv_ln_g=conv_ln_g, conv_ln_b=conv_ln_b, conv_w_o=conv_w_o, gqa_q_gain=gqa_q_gain,
             gqa_k_gain=gqa_k_gain, gqa_w_o=gqa_w_o, w_out=w_out, router_w=router_w, router_bias=router_bias,
             moe_w_gate=moe_w_gate, moe_w_up=moe_w_up, moe_w_down=moe_w_down)
    n = x.shape[1]
    rows = n // GRID_W
    row = jnp.repeat(jnp.arange(rows, dtype=jnp.int32), GRID_W)
    col = jnp.tile(jnp.arange(GRID_W, dtype=jnp.int32), rows)
    rope = (row, col)
    s_lat = jax.nn.silu(c)
    s_ctx = jax.nn.silu(c_ctx)
    xc = ctx
    for l in range(DEPTH):
        last = l == DEPTH - 1
        sh1, sc1, g1, sh2, sc2, g2 = jnp.split((s_lat @ w_mod[l] + b_mod[l])[:, None, :], 6, axis=-1)
        csh1, csc1, cg1, csh2, csc2, cg2 = jnp.split((s_ctx @ w_mod[l] + b_mod[l])[None, None, :], 6, axis=-1)
        h = rms_norm(x, norm1_g[l]) * (1 + sc1) + sh1
        hc = rms_norm(xc, norm1_g[l]) * (1 + csc1) + csh1
        q_mla, k_mla, v_mla, q_g, k_g, v_g, conv_in, gl = prepare(h, l, P, rope)
        cq_mla, ck_mla, cv_mla, cq_g, ck_g, cv_g, cconv_in, cgl = prepare(hc, l, P, None)
        k_mla_all = jnp.concatenate([k_mla, ck_mla], axis=1)
        v_mla_all = jnp.concatenate([v_mla, cv_mla], axis=1)
        k_g_all = jnp.concatenate([k_g, ck_g], axis=1)
        v_g_all = jnp.concatenate([v_g, cv_g], axis=1)
        x = x + g1 * mix(q_mla, q_g, conv_in, gl, k_mla_all, v_mla_all, k_g_all, v_g_all, l, P)
        if not last:
            xc = xc + cg1 * mix(cq_mla, cq_g, cconv_in, cgl, ck_mla, cv_mla, ck_g, cv_g, l, P)
        h2 = rms_norm(x, norm2_g[l]) * (1 + sc2) + sh2
        x = x + g2 * moe(h2, l, P)
        if not last:
            hc2 = rms_norm(xc, norm2_g[l]) * (1 + csc2) + csh2
            xc = xc + cg2 * moe(hc2, l, P)
    return x
```

```python
import functools
import math

import numpy as np
import jax
import jax.numpy as jnp
from jax import lax
from jax.experimental import pallas as pl
from jax.experimental.pallas import tpu as pltpu

GRID_W = 64
ROPE_BASE = 10000.0
EPS = 1e-6

MLA_HEADS = 8
MLA_Q_RANK = 256
MLA_KV_RANK = 128
MLA_NOPE = 64
MLA_ROPE = 32
MLA_V = 64
MLA_QK = MLA_NOPE + MLA_ROPE
CONV_CH = 512
CONV_W = 31
GQA_HEADS = 8
GQA_KV_HEADS = 2
GQA_HD = 64
N_BRANCH = 3
N_EXPERTS = 16
N_GROUPS = 4
EXPERTS_PER_GROUP = 4
D_EXPERT = 512

LANES = 128
HALF = LANES // 2
N_PAIRS = 6
N_BUCKETS = N_GROUPS * N_PAIRS
BUCKET_ROWS = 32
LOG2E = math.log2(math.e)
VMEM_LIMIT = 56 * 1024 * 1024

C_CQ = 0
C_CKV = C_CQ + MLA_Q_RANK
C_KPE = C_CKV + MLA_KV_RANK
C_GK = C_KPE + LANES
C_GV = C_GK + LANES
C_GQ = C_GV + LANES
C_CA = C_GQ + GQA_HEADS * LANES
C_CG = C_CA + CONV_CH
C_GATE = C_CG + CONV_CH
D_IN_R = C_GATE

V_QN = 0
V_KVN = 256
V_QG = 384
V_KG = 512
V_GGQ = 640
V_GGK = 768

BF16 = jnp.bfloat16
F32 = jnp.float32
NT_DIMS = (((1,), (1,)), ((), ()))


def _dot(a, b):
    return jnp.dot(a, b, preferred_element_type=F32)


def _split_hi_lo(a):
    hi = a.astype(BF16)
    lo = (a - hi.astype(F32)).astype(BF16)
    return hi, lo


def _params(*sem):
    return pltpu.CompilerParams(dimension_semantics=sem, vmem_limit_bytes=VMEM_LIMIT)


def _const_spec(shape):
    nd = len(shape)
    return pl.BlockSpec(shape, lambda *_: (0,) * nd)


def _mod_kernel(c_ref, w_ref, b_ref, o_ref):
    c = c_ref[...]
    s = c * jax.nn.sigmoid(c)
    sh, sl = _split_hi_lo(s)
    wh, wl = _split_hi_lo(w_ref[0])
    o_ref[0] = _dot(sh, wh) + _dot(sh, wl) + _dot(sl, wh) + b_ref[0]


def _modulation(c_all, w_mod, b_mod):
    depth, d, d6 = w_mod.shape
    m = c_all.shape[0]
    tn = 1536
    return pl.pallas_call(
        _mod_kernel,
        grid=(depth, d6 // tn),
        in_specs=[pl.BlockSpec((m, d), lambda l, j: (0, 0)),
                  pl.BlockSpec((1, d, tn), lambda l, j: (l, 0, j)),
                  pl.BlockSpec((1, 1, tn), lambda l, j: (l, 0, j))],
        out_specs=pl.BlockSpec((1, m, tn), lambda l, j: (l, 0, j)),
        out_shape=jax.ShapeDtypeStruct((depth, m, d6), F32),
        compiler_params=_params("parallel", "parallel"),
        name="modulation",
    )(c_all, w_mod, b_mod.reshape(depth, 1, d6))


def _rope(n, cos, sa, sb, quarter):
    return n * cos + pltpu.roll(n, LANES - quarter, 1) * sa + pltpu.roll(n, quarter, 1) * sb


def _pre_mixer_kernel(x_ref, sc_ref, sh_ref, vec_ref, w_ref, wuq_ref, wukv_ref, *rest, use_rope, d_model):
    if use_rope:
        mc_ref, msa_ref, msb_ref, gc_ref, gsa_ref, gsb_ref = rest[:6]
        rest = rest[6:]
    qm_ref, km_ref, vm_ref, qg_ref, kg_ref, vg_ref, y_ref, gate_ref = rest

    x = x_ref[0]
    inv = lax.rsqrt(jnp.mean(x * x, axis=-1, keepdims=True) + EPS)
    h = (x * inv * vec_ref[0:1, :]) * (1.0 + sc_ref[0]) + sh_ref[0]
    hb = h.astype(BF16)

    lane = lax.broadcasted_iota(jnp.int32, (1, LANES), 1)
    lo = lane < HALF

    def proj(c0, width):
        return _dot(hb, w_ref[:, c0:c0 + width])

    def rms(v, g):
        return v * lax.rsqrt(jnp.mean(v * v, axis=-1, keepdims=True) + EPS) * g

    def seg_inv(xs, n_lo, n_hi):
        sq = xs * xs
        if n_hi is None:
            return lax.rsqrt(jnp.sum(sq, axis=-1, keepdims=True) * (1.0 / n_lo) + EPS)
        s_lo = jnp.sum(jnp.where(lo, sq, 0.0), axis=-1, keepdims=True)
        s_hi = jnp.sum(jnp.where(lo, 0.0, sq), axis=-1, keepdims=True)
        return jnp.where(lo, lax.rsqrt(s_lo * (1.0 / n_lo) + EPS), lax.rsqrt(s_hi * (1.0 / n_hi) + EPS))

    def mla_rope(n):
        return _rope(n, mc_ref[...], msa_ref[...], msb_ref[...], MLA_ROPE // 4) if use_rope else n

    def gqa_rope(n):
        return _rope(n, gc_ref[...], gsa_ref[...], gsb_ref[...], GQA_HD // 4) if use_rope else n

    cqn = rms(proj(C_CQ, MLA_Q_RANK), vec_ref[1:2, V_QN:V_QN + MLA_Q_RANK])
    q = _dot(cqn.astype(BF16), wuq_ref[...])
    qgain = vec_ref[1:2, V_QG:V_QG + LANES]
    for hh in range(MLA_HEADS):
        xs = q[:, hh * LANES:(hh + 1) * LANES]
        n = mla_rope(xs * seg_inv(xs, MLA_NOPE, MLA_ROPE) * qgain)
        qm_ref[0, hh] = (n * (MLA_QK ** -0.5 * LOG2E)).astype(BF16)

    kgain = vec_ref[1:2, V_KG:V_KG + LANES]
    kpe = proj(C_KPE, LANES)
    kpe = mla_rope(kpe * seg_inv(kpe, MLA_ROPE, None) * kgain)
    ckvn = rms(proj(C_CKV, MLA_KV_RANK), vec_ref[1:2, V_KVN:V_KVN + MLA_KV_RANK])
    kv = _dot(ckvn.astype(BF16), wukv_ref[...])
    for hh in range(MLA_HEADS):
        xs = kv[:, hh * LANES:(hh + 1) * LANES]
        km_ref[0, hh] = (xs * seg_inv(xs, MLA_NOPE, None) * kgain + kpe).astype(BF16)
    v0 = MLA_HEADS * LANES
    for p in range(MLA_HEADS // 2):
        vm_ref[0, p] = kv[:, v0 + p * LANES:v0 + (p + 1) * LANES].astype(BF16)

    gk = proj(C_GK, LANES)
    kg_ref[0] = gqa_rope(gk * seg_inv(gk, GQA_HD, GQA_HD) * vec_ref[1:2, V_GGK:V_GGK + LANES]).astype(BF16)
    vg_ref[0] = proj(C_GV, LANES).astype(BF16)
    ggq = vec_ref[1:2, V_GGQ:V_GGQ + LANES]
    for hh in range(GQA_HEADS):
        xs = proj(C_GQ + hh * LANES, LANES)
        n = gqa_rope(xs * seg_inv(xs, GQA_HD, None) * ggq)
        qg_ref[0, hh] = (n * (GQA_HD ** -0.5 * LOG2E)).astype(BF16)

    y_ref[0] = proj(C_CA, CONV_CH) * jax.nn.sigmoid(proj(C_CG, CONV_CH))
    for j in range(N_BRANCH):
        gate_ref[0, :, j * d_model:(j + 1) * d_model] = jax.nn.sigmoid(
            proj(C_GATE + j * d_model, d_model)).astype(BF16)


def _pre_mixer(x, sc, sh, vec, w_in_r, w_uq_r, w_ukv_r, tables, tm):
    b, n, d = x.shape
    use_rope = tables is not None
    nt = n // tm
    bmap = lambda i, j: (j, 0, 0)
    in_specs = [pl.BlockSpec((1, tm, d), lambda i, j: (j, i, 0)),
                pl.BlockSpec((1, 1, d), bmap),
                pl.BlockSpec((1, 1, d), bmap),
                _const_spec(vec.shape), _const_spec(w_in_r.shape), _const_spec(w_uq_r.shape),
                _const_spec(w_ukv_r.shape)]
    args = [x, sc, sh, vec, w_in_r, w_uq_r, w_ukv_r]
    if use_rope:
        in_specs += [pl.BlockSpec((tm, LANES), lambda i, j: (i, 0))] * 6
        args += list(tables)
    hspec = lambda nh: pl.BlockSpec((1, nh, tm, LANES), lambda i, j: (j, 0, i, 0))
    tspec = lambda w: pl.BlockSpec((1, tm, w), lambda i, j: (j, i, 0))
    out_specs = [hspec(MLA_HEADS), hspec(MLA_HEADS), hspec(MLA_HEADS // 2), hspec(GQA_HEADS),
                 tspec(LANES), tspec(LANES), tspec(CONV_CH), tspec(N_BRANCH * d)]
    sds = jax.ShapeDtypeStruct
    out_shape = [sds((b, MLA_HEADS, n, LANES), BF16), sds((b, MLA_HEADS, n, LANES), BF16),
                 sds((b, MLA_HEADS // 2, n, LANES), BF16), sds((b, GQA_HEADS, n, LANES), BF16),
                 sds((b, n, LANES), BF16), sds((b, n, LANES), BF16),
                 sds((b, n, CONV_CH), F32), sds((b, n, N_BRANCH * d), BF16)]
    return pl.pallas_call(
        functools.partial(_pre_mixer_kernel, use_rope=use_rope, d_model=d),
        grid=(nt, b), in_specs=in_specs, out_specs=out_specs, out_shape=out_shape,
        compiler_params=_params("parallel", "parallel"),
        name="pre_mixer_rope" if use_rope else "pre_mixer",
    )(*args)


def _attn_kernel(q_ref, *refs, n_seg, head_map, pairs):
    kv_refs = refs[:2 * n_seg]
    o_ref = refs[2 * n_seg]
    lane = lax.broadcasted_iota(jnp.int32, (1, LANES), 1)
    lo = lane < HALF

    def one_head(hh):
        kidx, vidx = head_map[hh]
        q = q_ref[0, hh]
        ss = [lax.dot_general(q, kv_refs[2 * s][0, kidx], NT_DIMS, preferred_element_type=F32)
              for s in range(n_seg)]
        m = functools.reduce(jnp.maximum, [jnp.max(s, axis=-1, keepdims=True) for s in ss])
        ps = [jnp.exp2(s - m) for s in ss]
        l = functools.reduce(jnp.add, [jnp.sum(p, axis=-1, keepdims=True) for p in ps])
        o = functools.reduce(jnp.add, [_dot(ps[s].astype(BF16), kv_refs[2 * s + 1][0, vidx])
                                       for s in range(n_seg)])
        return o / l

    for p, (h_lo, h_hi) in enumerate(pairs):
        o_ref[0, :, p * LANES:(p + 1) * LANES] = jnp.where(lo, one_head(h_lo), one_head(h_hi)).astype(BF16)


MLA_HEAD_MAP = tuple((h, h // 2) for h in range(MLA_HEADS))
MLA_PAIRS = tuple((2 * p, 2 * p + 1) for p in range(MLA_HEADS // 2))
GQA_HEAD_MAP = tuple((0, 0) for _ in range(GQA_HEADS))
GQA_PAIRS = tuple((a, a + GQA_HEADS // 2) for a in range(GQA_HEADS // 2))


def _attention(q, kvs, head_map, pairs, tq, name):
    b, nh, n, _ = q.shape
    in_specs = [pl.BlockSpec((1, nh, tq, LANES), lambda i, j: (i, 0, j, 0))]
    args = [q]
    for a in kvs:
        in_specs.append(pl.BlockSpec((1,) + a.shape[1:], lambda i, j: (i, 0, 0, 0)))
        args.append(a)
    width = len(pairs) * LANES
    return pl.pallas_call(
        functools.partial(_attn_kernel, n_seg=len(kvs) // 2, head_map=head_map, pairs=pairs),
        grid=(b, n // tq), in_specs=in_specs,
        out_specs=pl.BlockSpec((1, tq, width), lambda i, j: (i, j, 0)),
        out_shape=jax.ShapeDtypeStruct((b, n, width), BF16),
        compiler_params=_params("parallel", "parallel"),
        name=name,
    )(*args)


CONV_PAD = 16
CONV_TAIL = 24
CONV_CHUNK = 128


def _conv_kernel(y_ref, cw_ref, vec_ref, z_ref, buf_ref, *, n):
    ch = y_ref.shape[-1]
    buf_ref[0:CONV_PAD, :] = jnp.zeros((CONV_PAD, ch), F32)
    buf_ref[CONV_PAD + n:CONV_PAD + n + CONV_TAIL, :] = jnp.zeros((CONV_TAIL, ch), F32)
    buf_ref[CONV_PAD:CONV_PAD + n, :] = y_ref[0]
    first = CONV_PAD - CONV_W // 2
    win_rows = CONV_CHUNK + CONV_W + 1

    def chunk(c, carry):
        r0 = pl.multiple_of(c * CONV_CHUNK, CONV_CHUNK)
        win = buf_ref[pl.ds(r0, win_rows), :]
        acc = jnp.zeros((CONV_CHUNK, ch), F32) + vec_ref[0:1, :]
        for k in range(CONV_W):
            acc = acc + win[first + k:first + k + CONV_CHUNK, :] * cw_ref[k:k + 1, :]
        mu = jnp.mean(acc, axis=-1, keepdims=True)
        cen = acc - mu
        var = jnp.mean(cen * cen, axis=-1, keepdims=True)
        yn = cen * lax.rsqrt(var + EPS) * vec_ref[1:2, :] + vec_ref[2:3, :]
        z_ref[0, pl.ds(r0, CONV_CHUNK), :] = (yn * jax.nn.sigmoid(yn)).astype(BF16)
        return carry

    lax.fori_loop(0, n // CONV_CHUNK, chunk, 0)


def _conv(y, cw, cvec):
    b, n, ch = y.shape
    return pl.pallas_call(
        functools.partial(_conv_kernel, n=n),
        grid=(b,),
        in_specs=[pl.BlockSpec((1, n, ch), lambda i: (i, 0, 0)), _const_spec(cw.shape), _const_spec(cvec.shape)],
        out_specs=pl.BlockSpec((1, n, ch), lambda i: (i, 0, 0)),
        out_shape=jax.ShapeDtypeStruct((b, n, ch), BF16),
        scratch_shapes=[pltpu.VMEM((CONV_PAD + n + CONV_TAIL, ch), F32)],
        compiler_params=_params("parallel"),
        name="conv",
    )(y, cw, cvec)


def _merge_kernel(x_ref, g_ref, om_ref, oc_ref, og_ref, gate_ref, wm_ref, wc_ref, wg_ref, wo_ref, o_ref, *, d_model):
    d = d_model
    merged = (gate_ref[0, :, 0:d].astype(F32) * _dot(om_ref[0], wm_ref[...])
              + gate_ref[0, :, d:2 * d].astype(F32) * _dot(oc_ref[0], wc_ref[...])
              + gate_ref[0, :, 2 * d:3 * d].astype(F32) * _dot(og_ref[0], wg_ref[...]))
    o_ref[0] = x_ref[0] + g_ref[0] * _dot(merged.astype(BF16), wo_ref[...])


def _merge(x, g1, o_mla, z, o_gqa, gates, wm, wc, wg, wo, tm):
    b, n, d = x.shape
    tok = lambda w: pl.BlockSpec((1, tm, w), lambda i, j: (i, j, 0))
    return pl.pallas_call(
        functools.partial(_merge_kernel, d_model=d),
        grid=(b, n // tm),
        in_specs=[tok(d), pl.BlockSpec((1, 1, d), lambda i, j: (i, 0, 0)), tok(o_mla.shape[-1]), tok(z.shape[-1]),
                  tok(o_gqa.shape[-1]), tok(N_BRANCH * d), _const_spec(wm.shape), _const_spec(wc.shape),
                  _const_spec(wg.shape), _const_spec(wo.shape)],
        out_specs=tok(d),
        out_shape=jax.ShapeDtypeStruct((b, n, d), F32),
        compiler_params=_params("parallel", "parallel"),
        name="merge",
    )(x, g1, o_mla, z, o_gqa, gates, wm, wc, wg, wo)


def _first_argmax4(v):
    m = functools.reduce(jnp.maximum, v)
    idx = jnp.where(v[0] == m, 0, jnp.where(v[1] == m, 1, jnp.where(v[2] == m, 2, 3)))
    return m, idx


def _pick4(idx, v):
    return jnp.where(idx == 0, v[0], jnp.where(idx == 1, v[1], jnp.where(idx == 2, v[2], v[3])))


def _moe_pre_kernel(x_ref, sc_ref, sh_ref, g_ref, rw_ref, rb_ref, cin_ref, hx_ref, bkt_ref, rank_ref, cnt_ref,
                    *, d_model):
    tm = x_ref.shape[1]
    first = (pl.program_id(0) == 0) & (pl.program_id(1) == 0)

    @pl.when(first)
    def _():
        cnt_ref[...] = cin_ref[...]

    x = x_ref[0]
    inv = lax.rsqrt(jnp.mean(x * x, axis=-1, keepdims=True) + EPS)
    h = (x * inv * g_ref[...]) * (1.0 + sc_ref[0]) + sh_ref[0]
    hx_ref[0, :, 0:d_model] = h

    hh, hl = _split_hi_lo(h)
    rwh, rwl = rw_ref[0], rw_ref[1]
    logits = (lax.dot_general(rwh, hh, NT_DIMS, preferred_element_type=F32)
              + lax.dot_general(rwh, hl, NT_DIMS, preferred_element_type=F32)
              + lax.dot_general(rwl, hh, NT_DIMS, preferred_element_type=F32))
    s_all = jax.nn.sigmoid(logits)
    sel_all = s_all + rb_ref[...]
    s = [s_all[e:e + 1, :] for e in range(N_EXPERTS)]
    sel = [sel_all[e:e + 1, :] for e in range(N_EXPERTS)]

    gscore = []
    for g in range(N_GROUPS):
        a, b, c, d = sel[4 * g:4 * g + 4]
        gscore.append(functools.reduce(jnp.maximum, [a + b, a + c, a + d, b + c, b + d, c + d]))
    _, gi = _first_argmax4(gscore)
    v = [_pick4(gi, [sel[4 * g + j] for g in range(N_GROUPS)]) for j in range(EXPERTS_PER_GROUP)]
    u = [_pick4(gi, [s[4 * g + j] for g in range(N_GROUPS)]) for j in range(EXPERTS_PER_GROUP)]
    _, i1 = _first_argmax4(v)
    _, i2 = _first_argmax4([jnp.where(i1 == j, -jnp.inf, v[j]) for j in range(EXPERTS_PER_GROUP)])
    e_lo = jnp.minimum(i1, i2)
    e_hi = jnp.maximum(i1, i2)
    pair = jnp.where(e_lo == 0, e_hi - 1, jnp.where(e_lo == 1, e_hi + 1, N_PAIRS - 1))
    bucket = gi * N_PAIRS + pair
    u_lo = _pick4(e_lo, u)
    u_hi = _pick4(e_hi, u)
    den = u_lo + u_hi
    wts = jnp.concatenate([u_lo / den, u_hi / den, jnp.zeros((LANES - 2, tm), F32)], axis=0)
    hx_ref[0, :, d_model:d_model + LANES] = wts.T

    onehot = lax.broadcasted_iota(jnp.int32, (BUCKET_ROWS, tm), 0) == bucket
    oh = jnp.where(onehot, 1.0, 0.0)
    before = lax.broadcasted_iota(jnp.int32, (tm, tm), 0) < lax.broadcasted_iota(jnp.int32, (tm, tm), 1)
    cum = _dot(oh.astype(BF16), jnp.where(before, 1.0, 0.0).astype(BF16))
    rank = jnp.sum(oh * (cum + cnt_ref[:, 0:1]), axis=0, keepdims=True)
    bkt_ref[0, 0] = bucket
    rank_ref[0, 0] = rank.astype(jnp.int32)
    cnt_ref[...] = cnt_ref[...] + jnp.sum(oh, axis=1, keepdims=True)


def _moe_pre(x, sc, sh, g, rw, rb, cin, tm):
    b, n, d = x.shape
    nt = n // tm
    sds = jax.ShapeDtypeStruct
    ispec = pl.BlockSpec((1, 1, 1, tm), lambda i, j: (i, j, 0, 0))
    bmap = lambda i, j: (i, 0, 0)
    return pl.pallas_call(
        functools.partial(_moe_pre_kernel, d_model=d),
        grid=(b, nt),
        in_specs=[pl.BlockSpec((1, tm, d), lambda i, j: (i, j, 0)), pl.BlockSpec((1, 1, d), bmap),
                  pl.BlockSpec((1, 1, d), bmap), _const_spec(g.shape), _const_spec(rw.shape),
                  _const_spec(rb.shape), _const_spec(cin.shape)],
        out_specs=[pl.BlockSpec((1, tm, d + LANES), lambda i, j: (i, j, 0)), ispec, ispec,
                   _const_spec((BUCKET_ROWS, LANES))],
        out_shape=[sds((b, n, d + LANES), F32), sds((b, nt, 1, tm), jnp.int32), sds((b, nt, 1, tm), jnp.int32),
                   sds((BUCKET_ROWS, LANES), F32)],
        compiler_params=_params("arbitrary", "arbitrary"),
        name="moe_pre",
    )(x, sc, sh, g, rw, rb, cin)


ROW_UNROLL = 8


def _scatter_kernel(pos_ref, src_ref, hs_in_ref, hs_ref, sem):
    del hs_in_ref
    tm = src_ref.shape[0]

    def row_copy(r, p):
        return pltpu.make_async_copy(src_ref.at[pl.ds(r, 1)], hs_ref.at[pl.ds(p, 1)], sem)

    def issue(r, c):
        row_copy(r, pos_ref[0, 0, r]).start()
        return c

    def drain(r, c):
        row_copy(0, 0).wait()
        return c

    lax.fori_loop(0, tm, issue, 0, unroll=ROW_UNROLL)
    lax.fori_loop(0, tm, drain, 0, unroll=ROW_UNROLL)


def _scatter(pos3, src, hs, tm):
    t, w = src.shape
    return pl.pallas_call(
        _scatter_kernel,
        grid=(t // tm,),
        in_specs=[pl.BlockSpec((1, 1, tm), lambda i: (i, 0, 0), memory_space=pltpu.SMEM),
                  pl.BlockSpec((tm, w), lambda i: (i, 0)),
                  pl.BlockSpec(memory_space=pl.ANY)],
        out_specs=pl.BlockSpec(memory_space=pl.ANY),
        out_shape=jax.ShapeDtypeStruct(hs.shape, hs.dtype),
        scratch_shapes=[pltpu.SemaphoreType.DMA(())],
        input_output_aliases={2: 0},
        compiler_params=_params("arbitrary"),
        name="scatter_rows",
    )(pos3, src, hs)


def _gather_res_kernel(pos_ref, x_ref, g_ref, ys_ref, o_ref, buf_ref, sem):
    tm = x_ref.shape[1]

    def row_copy(r, p):
        return pltpu.make_async_copy(ys_ref.at[pl.ds(p, 1)], buf_ref.at[pl.ds(r, 1)], sem)

    def issue(r, c):
        row_copy(r, pos_ref[0, 0, r]).start()
        return c

    def drain(r, c):
        row_copy(0, 0).wait()
        return c

    lax.fori_loop(0, tm, issue, 0, unroll=ROW_UNROLL)
    lax.fori_loop(0, tm, drain, 0, unroll=ROW_UNROLL)
    o_ref[0] = x_ref[0] + g_ref[0] * buf_ref[...]


def _gather_res(pos3, x, g2, ys, tm):
    b, n, d = x.shape
    nt = n // tm
    return pl.pallas_call(
        _gather_res_kernel,
        grid=(b, nt),
        in_specs=[pl.BlockSpec((1, 1, tm), lambda i, j: (i * nt + j, 0, 0), memory_space=pltpu.SMEM),
                  pl.BlockSpec((1, tm, d), lambda i, j: (i, j, 0)),
                  pl.BlockSpec((1, 1, d), lambda i, j: (i, 0, 0)),
                  pl.BlockSpec(memory_space=pl.ANY)],
        out_specs=pl.BlockSpec((1, tm, d), lambda i, j: (i, j, 0)),
        out_shape=jax.ShapeDtypeStruct((b, n, d), F32),
        scratch_shapes=[pltpu.VMEM((tm, d), F32), pltpu.SemaphoreType.DMA(())],
        compiler_params=_params("arbitrary", "arbitrary"),
        name="gather_residual",
    )(pos3, x, g2, ys)


def _experts_kernel(e1_ref, e2_ref, blk_ref, valid_ref, hs_ref, wg1_ref, wu1_ref, wd1_ref, wg2_ref, wu2_ref,
                    wd2_ref, o_ref, *, d_model):
    valid = valid_ref[pl.program_id(0)] > 0

    @pl.when(jnp.logical_not(valid))
    def _():
        o_ref[...] = jnp.zeros(o_ref.shape, F32)

    @pl.when(valid)
    def _():
        xb = hs_ref[:, 0:d_model].astype(BF16)

        def half(wg_ref, wu_ref, w):
            gte = _dot(xb, wg_ref[0])
            return (gte * jax.nn.sigmoid(gte) * _dot(xb, wu_ref[0]) * w).astype(BF16)

        h1 = half(wg1_ref, wu1_ref, hs_ref[:, d_model:d_model + 1])
        h2 = half(wg2_ref, wu2_ref, hs_ref[:, d_model + 1:d_model + 2])
        o_ref[...] = _dot(h1, wd1_ref[0]) + _dot(h2, wd2_ref[0])


def _experts(e1, e2, blk, valid, hs, wg, wu, wd, tm):
    tp, w = hs.shape
    d = w - LANES
    f = wg.shape[-1]
    up1 = pl.BlockSpec((1, d, f), lambda i, e1, e2, blk, valid: (e1[i], 0, 0))
    up2 = pl.BlockSpec((1, d, f), lambda i, e1, e2, blk, valid: (e2[i], 0, 0))
    dn1 = pl.BlockSpec((1, f, d), lambda i, e1, e2, blk, valid: (e1[i], 0, 0))
    dn2 = pl.BlockSpec((1, f, d), lambda i, e1, e2, blk, valid: (e2[i], 0, 0))
    grid_spec = pltpu.PrefetchScalarGridSpec(
        num_scalar_prefetch=4, grid=(tp // tm,),
        in_specs=[pl.BlockSpec((tm, w), lambda i, e1, e2, blk, valid: (blk[i], 0)), up1, up1, dn1, up2, up2, dn2],
        out_specs=pl.BlockSpec((tm, d), lambda i, e1, e2, blk, valid: (i, 0)))
    return pl.pallas_call(
        functools.partial(_experts_kernel, d_model=d),
        grid_spec=grid_spec,
        out_shape=jax.ShapeDtypeStruct((tp, d), F32),
        compiler_params=_params("arbitrary"),
        name="moe_experts",
    )(e1, e2, blk, valid, hs, wg, wu, wd, wg, wu, wd)


PAIR_LO = np.array([0, 0, 0, 1, 1, 2], np.int32)
PAIR_HI = np.array([1, 2, 3, 2, 3, 3], np.int32)


def _moe_plan(counts, n_tiles, tm):
    c = counts[:N_BUCKETS]
    nt = (c + tm - 1) // tm
    tile_end = jnp.cumsum(nt)
    row_off = (tile_end - nt) * tm
    n_valid = tile_end[-1]
    i = jnp.arange(n_tiles, dtype=jnp.int32)
    blk = jnp.minimum(i, n_valid - 1)
    bkt = jnp.minimum(jnp.searchsorted(tile_end, blk, side="right"), N_BUCKETS - 1).astype(jnp.int32)
    grp = bkt // N_PAIRS
    e1 = grp * EXPERTS_PER_GROUP + jnp.asarray(PAIR_LO)[bkt % N_PAIRS]
    e2 = grp * EXPERTS_PER_GROUP + jnp.asarray(PAIR_HI)[bkt % N_PAIRS]
    return row_off, e1, e2, blk.astype(jnp.int32), (i < n_valid).astype(jnp.int32)


def _moe(parts, rw, rb, wg, wu, wd, tm_exp):
    cin = jnp.zeros((BUCKET_ROWS, LANES), F32)
    pre = []
    for x, sc, sh, g2, gain, tm in parts:
        hx, bkt, rank, cin = _moe_pre(x, sc, sh, gain, rw, rb, cin, tm)
        pre.append((hx, bkt, rank))
    total = sum(p[0].shape[0] * p[0].shape[1] for p in parts)
    n_tiles = -(-total // tm_exp) + N_BUCKETS
    counts = cin[:, 0].astype(jnp.int32)
    row_off, e1, e2, blk, valid = _moe_plan(counts, n_tiles, tm_exp)
    w = pre[0][0].shape[-1]
    hs = jnp.zeros((n_tiles * tm_exp, w), F32)
    poss = []
    for (x, sc, sh, g2, gain, tm), (hx, bkt, rank) in zip(parts, pre):
        pos = (row_off[bkt] + rank).reshape(-1, 1, tm)
        poss.append(pos)
        hs = _scatter(pos, hx.reshape(-1, w), hs, tm)
    ys = _experts(e1, e2, blk, valid, hs, wg, wu, wd, tm_exp)
    return [_gather_res(pos, x, g2, ys, tm) for (x, sc, sh, g2, gain, tm), pos in zip(parts, poss)]


def _relayout_w_in(w_in, d_model):
    sizes = (MLA_Q_RANK, MLA_KV_RANK, MLA_ROPE, 2 * CONV_CH, GQA_HEADS * GQA_HD, GQA_KV_HEADS * GQA_HD,
             GQA_KV_HEADS * GQA_HD, N_BRANCH * d_model)
    offs = np.cumsum((0,) + sizes)
    cq, ckv, kpe, conv, gq, gk, gv, gates = [w_in[..., offs[i]:offs[i + 1]] for i in range(len(sizes))]
    lead = w_in.shape[:-1]
    z = lambda n: jnp.zeros(lead + (n,), w_in.dtype)
    per_kv = GQA_HEADS // GQA_KV_HEADS
    gq_groups = []
    for hh in range(GQA_HEADS):
        col = gq[..., hh * GQA_HD:(hh + 1) * GQA_HD]
        gq_groups += [col, z(HALF)] if hh // per_kv == 0 else [z(HALF), col]
    out = jnp.concatenate([cq, ckv, z(HALF), kpe, z(HALF - MLA_ROPE), gk, gv] + gq_groups + [conv, gates], axis=-1)
    return out.astype(BF16)


def _relayout_uq(w_uq):
    l, r, _ = w_uq.shape
    w = w_uq.reshape(l, r, MLA_HEADS, MLA_QK)
    w = jnp.pad(w, ((0, 0), (0, 0), (0, 0), (0, LANES - MLA_QK)))
    return w.reshape(l, r, MLA_HEADS * LANES).astype(BF16)


def _relayout_ukv(w_ukv):
    l, r, _ = w_ukv.shape
    w = w_ukv.reshape(l, r, MLA_HEADS, MLA_NOPE + MLA_V)
    k = jnp.pad(w[..., :MLA_NOPE], ((0, 0), (0, 0), (0, 0), (0, LANES - MLA_NOPE)))
    v = w[..., MLA_NOPE:]
    return jnp.concatenate([k.reshape(l, r, MLA_HEADS * LANES), v.reshape(l, r, MLA_HEADS * MLA_V)],
                           axis=-1).astype(BF16)


def _rope_tables(n, r):
    rows = n // GRID_W
    row = jnp.repeat(jnp.arange(rows, dtype=jnp.int32), GRID_W).astype(F32)
    col = jnp.tile(jnp.arange(GRID_W, dtype=jnp.int32), rows).astype(F32)
    dim = r // 2
    inv = ROPE_BASE ** (-jnp.arange(0, dim, 2, dtype=F32) / dim)
    ar, ac = row[:, None] * inv[None, :], col[:, None] * inv[None, :]
    zero = jnp.zeros_like(ar)
    cos = jnp.concatenate([jnp.cos(ar), jnp.cos(ar), jnp.cos(ac), jnp.cos(ac)], axis=-1)
    sa = jnp.concatenate([-jnp.sin(ar), zero, -jnp.sin(ac), zero], axis=-1)
    sb = jnp.concatenate([zero, jnp.sin(ar), zero, jnp.sin(ac)], axis=-1)
    return cos, sa, sb


def _all_tables(n):
    mc, msa, msb = _rope_tables(n, MLA_ROPE)
    one = jnp.ones((n, MLA_NOPE), F32)
    zero = jnp.zeros((n, MLA_NOPE), F32)
    pad1 = jnp.ones((n, LANES - MLA_QK), F32)
    pad0 = jnp.zeros((n, LANES - MLA_QK), F32)
    mla = (jnp.concatenate([one, mc, pad1], -1), jnp.concatenate([zero, msa, pad0], -1),
           jnp.concatenate([zero, msb, pad0], -1))
    gqa = tuple(jnp.concatenate([t, t], -1) for t in _rope_tables(n, GQA_HD))
    return mla + gqa


def kernel(x, c, ctx, c_ctx, w_mod, b_mod, norm1_g, norm2_g, w_in, mla_q_norm, mla_kv_norm, mla_w_uq, mla_w_ukv,
           mla_q_gain, mla_k_gain, mla_w_o, conv_w, conv_b, conv_ln_g, conv_ln_b, conv_w_o, gqa_q_gain, gqa_k_gain,
           gqa_w_o, w_out, router_w, router_bias, moe_w_gate, moe_w_up, moe_w_down):
    b, n, d = x.shape
    n_ctx = ctx.shape[1]
    depth = w_mod.shape[0]
    assert n % GRID_W == 0 and n % 512 == 0 and n_ctx % CONV_CHUNK == 0 and d % LANES == 0

    tm = 512
    tm_ctx = min(n_ctx, 256)
    tq = 256
    tm_exp = 512

    m_rows = -(-(b + 1) // 8) * 8
    c_all = jnp.concatenate([c, c_ctx[None, :], jnp.zeros((m_rows - b - 1, d), F32)], axis=0)
    mod = _modulation(c_all, w_mod, b_mod)

    w_in_r = _relayout_w_in(w_in, d)
    w_uq_r = _relayout_uq(mla_w_uq)
    w_ukv_r = _relayout_ukv(mla_w_ukv)
    zl = lambda k: jnp.zeros((depth, k), F32)
    qg128 = jnp.concatenate([mla_q_gain, zl(LANES - MLA_QK)], -1)
    kg128 = jnp.concatenate([mla_k_gain, zl(LANES - MLA_QK)], -1)
    row1 = jnp.concatenate([mla_q_norm, mla_kv_norm, qg128, kg128, gqa_q_gain, gqa_q_gain, gqa_k_gain, gqa_k_gain,
                            zl(d - V_GGK - LANES)], -1)
    vec = jnp.stack([norm1_g, row1], axis=1)
    gqa_perm = np.concatenate([np.arange(GQA_HD) + GQA_HD * hh for pair in GQA_PAIRS for hh in pair])
    wm_o = mla_w_o.astype(BF16)
    wg_o = gqa_w_o[:, gqa_perm, :].astype(BF16)
    wc_o = conv_w_o.astype(BF16)
    wo = w_out.astype(BF16)
    cw = jnp.pad(conv_w, ((0, 0), (0, 32 - CONV_W), (0, 0)))
    cvec = jnp.pad(jnp.stack([conv_b, conv_ln_g, conv_ln_b], axis=1), ((0, 0), (0, 5), (0, 0)))
    rwt = router_w.T
    rw_hi = rwt.astype(BF16)
    rw = jnp.stack([rw_hi, (rwt - rw_hi.astype(F32)).astype(BF16)])
    rb = router_bias.reshape(N_EXPERTS, 1)
    wg_e, wu_e, wd_e = moe_w_gate.astype(BF16), moe_w_up.astype(BF16), moe_w_down.astype(BF16)
    tables = _all_tables(n)

    xc = ctx
    for l in range(depth):
        last = l == depth - 1
        sh1, sc1, g1, sh2, sc2, g2 = [t[:, None, :] for t in jnp.split(mod[l, :b], 6, axis=-1)]
        csh1, csc1, cg1, csh2, csc2, cg2 = [jnp.broadcast_to(t[None, :, :], (b, 1, d))
                                            for t in jnp.split(mod[l, b:b + 1], 6, axis=-1)]
        qm, km, vm, qg, kg, vg, y, gates = _pre_mixer(x, sc1, sh1, vec[l], w_in_r[l], w_uq_r[l], w_ukv_r[l],
                                                      tables, tm)
        cqm, ckm, cvm, cqg, ckg, cvg, cy, cgates = _pre_mixer(xc, csc1, csh1, vec[l], w_in_r[l], w_uq_r[l],
                                                              w_ukv_r[l], None, tm_ctx)
        kg4, vg4, ckg4, cvg4 = kg[:, None], vg[:, None], ckg[:, None], cvg[:, None]
        o_mla = _attention(qm, [km, vm, ckm, cvm], MLA_HEAD_MAP, MLA_PAIRS, tq, "attn_mla")
        o_gqa = _attention(qg, [kg4, vg4, ckg4, cvg4], GQA_HEAD_MAP, GQA_PAIRS, tq, "attn_gqa")
        z = _conv(y, cw[l], cvec[l])
        x = _merge(x, g1, o_mla, z, o_gqa, gates, wm_o[l], wc_o[l], wg_o[l], wo[l], tm)
        moe_parts = [(x, sc2, sh2, g2, norm2_g[l][None, :], tm)]
        if not last:
            co_mla = _attention(cqm, [ckm, cvm], MLA_HEAD_MAP, MLA_PAIRS, tm_ctx, "attn_mla_ctx")
            co_gqa = _attention(cqg, [ckg4, cvg4], GQA_HEAD_MAP, GQA_PAIRS, tm_ctx, "attn_gqa_ctx")
            cz = _conv(cy, cw[l], cvec[l])
            xc = _merge(xc, cg1, co_mla, cz, co_gqa, cgates, wm_o[l], wc_o[l], wg_o[l], wo[l], tm_ctx)
            moe_parts.append((xc, csc2, csh2, cg2, norm2_g[l][None, :], tm_ctx))
        outs = _moe(moe_parts, rw, rb, wg_e[l], wu_e[l], wd_e[l], tm_exp)
        x = outs[0]
        if not last:
            xc = outs[1]
    return x
```

```python
import functools
import math

import numpy as np
import jax
import jax.numpy as jnp
from jax import lax
from jax.experimental import pallas as pl
from jax.experimental.pallas import tpu as pltpu

GRID_W = 64
ROPE_BASE = 10000.0
EPS = 1e-6

MLA_HEADS = 8
MLA_Q_RANK = 256
MLA_KV_RANK = 128
MLA_NOPE = 64
MLA_ROPE = 32
MLA_V = 64
MLA_QK = MLA_NOPE + MLA_ROPE
CONV_CH = 512
CONV_W = 31
GQA_HEADS = 8
GQA_KV_HEADS = 2
GQA_HD = 64
N_BRANCH = 3
N_EXPERTS = 16
N_GROUPS = 4
EXPERTS_PER_GROUP = 4
D_EXPERT = 512

LANES = 128
SUBLANES = 8
HALF = LANES // 2
N_PAIRS = 6
N_BUCKETS = N_GROUPS * N_PAIRS
BUCKET_ROWS = 32
LOG2E = math.log2(math.e)
VMEM_LIMIT = 56 * 1024 * 1024

C_CQ = 0
C_CKV = C_CQ + MLA_Q_RANK
C_KPE = C_CKV + MLA_KV_RANK
C_GK = C_KPE + LANES
C_GV = C_GK + LANES
C_GQ = C_GV + LANES
C_CA = C_GQ + GQA_HEADS * GQA_HD
C_CG = C_CA + CONV_CH
C_GATE = C_CG + CONV_CH
D_IN_R = C_GATE

V_QN = 0
V_KVN = 256
V_QG = 384
V_KG = 512
V_GGQ = 640
V_GGK = 768

BF16 = jnp.bfloat16
F32 = jnp.float32
NT_DIMS = (((1,), (1,)), ((), ()))


def _dot(a, b):
    return jnp.dot(a, b, preferred_element_type=F32)


def _split_hi_lo(a):
    hi = a.astype(BF16)
    lo = (a - hi.astype(F32)).astype(BF16)
    return hi, lo


def _params(*sem):
    return pltpu.CompilerParams(dimension_semantics=sem, vmem_limit_bytes=VMEM_LIMIT)


def _const_spec(shape):
    nd = len(shape)
    return pl.BlockSpec(shape, lambda *_: (0,) * nd)


def _mod_kernel(c_ref, w_ref, b_ref, o_ref):
    c = c_ref[...]
    s = c * jax.nn.sigmoid(c)
    sh, sl = _split_hi_lo(s)
    wh, wl = _split_hi_lo(w_ref[0])
    o_ref[0] = _dot(sh, wh) + _dot(sh, wl) + _dot(sl, wh) + b_ref[0]


def _modulation(c_all, w_mod, b_mod):
    depth, d, d6 = w_mod.shape
    m = c_all.shape[0]
    tn = 1536
    return pl.pallas_call(
        _mod_kernel,
        grid=(depth, d6 // tn),
        in_specs=[pl.BlockSpec((m, d), lambda l, j: (0, 0)),
                  pl.BlockSpec((1, d, tn), lambda l, j: (l, 0, j)),
                  pl.BlockSpec((1, 1, tn), lambda l, j: (l, 0, j))],
        out_specs=pl.BlockSpec((1, m, tn), lambda l, j: (l, 0, j)),
        out_shape=jax.ShapeDtypeStruct((depth, m, d6), F32),
        compiler_params=_params("parallel", "parallel"),
        name="modulation",
    )(c_all, w_mod, b_mod.reshape(depth, 1, d6))


def _rope(n, cos, sa, sb, quarter):
    return n * cos + pltpu.roll(n, LANES - quarter, 1) * sa + pltpu.roll(n, quarter, 1) * sb


def _pre_mixer_kernel(x_ref, sc_ref, sh_ref, vec_ref, w_ref, wuq_ref, wukv_ref, *rest, use_rope, d_model):
    if use_rope:
        mc_ref, msa_ref, msb_ref, gc_ref, gsa_ref, gsb_ref = rest[:6]
        rest = rest[6:]
    qm_ref, km_ref, vm_ref, qg_ref, kg_ref, vg_ref, y_ref, gate_ref = rest

    x = x_ref[0]
    inv = lax.rsqrt(jnp.mean(x * x, axis=-1, keepdims=True) + EPS)
    h = (x * inv * vec_ref[0:1, :]) * (1.0 + sc_ref[0]) + sh_ref[0]
    hb = h.astype(BF16)

    lane = lax.broadcasted_iota(jnp.int32, (1, LANES), 1)
    lo = lane < HALF

    def proj(c0, width):
        return _dot(hb, w_ref[:, c0:c0 + width])

    def rms(v, g):
        return v * lax.rsqrt(jnp.mean(v * v, axis=-1, keepdims=True) + EPS) * g

    def seg_inv(xs, n_lo, n_hi):
        sq = xs * xs
        if n_hi is None:
            return lax.rsqrt(jnp.sum(sq, axis=-1, keepdims=True) * (1.0 / n_lo) + EPS)
        s_lo = jnp.sum(jnp.where(lo, sq, 0.0), axis=-1, keepdims=True)
        s_hi = jnp.sum(jnp.where(lo, 0.0, sq), axis=-1, keepdims=True)
        return jnp.where(lo, lax.rsqrt(s_lo * (1.0 / n_lo) + EPS), lax.rsqrt(s_hi * (1.0 / n_hi) + EPS))

    def mla_rope(n):
        return _rope(n, mc_ref[...], msa_ref[...], msb_ref[...], MLA_ROPE // 4) if use_rope else n

    def gqa_rope(n):
        return _rope(n, gc_ref[...], gsa_ref[...], gsb_ref[...], GQA_HD // 4) if use_rope else n

    small = proj(0, C_CA)

    def narrow(c0, width):
        return small[:, c0:c0 + width]

    cqn = rms(narrow(C_CQ, MLA_Q_RANK), vec_ref[1:2, V_QN:V_QN + MLA_Q_RANK])
    q = _dot(cqn.astype(BF16), wuq_ref[...])
    qgain = vec_ref[1:2, V_QG:V_QG + LANES]
    for hh in range(MLA_HEADS):
        xs = q[:, hh * LANES:(hh + 1) * LANES]
        n = mla_rope(xs * seg_inv(xs, MLA_NOPE, MLA_ROPE) * qgain)
        qm_ref[0, hh] = (n * (MLA_QK ** -0.5 * LOG2E)).astype(BF16)

    kgain = vec_ref[1:2, V_KG:V_KG + LANES]
    kpe = narrow(C_KPE, LANES)
    kpe = mla_rope(kpe * seg_inv(kpe, MLA_ROPE, None) * kgain)
    ckvn = rms(narrow(C_CKV, MLA_KV_RANK), vec_ref[1:2, V_KVN:V_KVN + MLA_KV_RANK])
    kv = _dot(ckvn.astype(BF16), wukv_ref[...])
    for hh in range(MLA_HEADS):
        xs = kv[:, hh * LANES:(hh + 1) * LANES]
        km_ref[0, hh] = (xs * seg_inv(xs, MLA_NOPE, None) * kgain + kpe).astype(BF16)
    v0 = MLA_HEADS * LANES
    for p in range(MLA_HEADS // 2):
        vm_ref[0, p] = kv[:, v0 + p * LANES:v0 + (p + 1) * LANES].astype(BF16)

    gk = narrow(C_GK, LANES)
    kg_ref[0] = gqa_rope(gk * seg_inv(gk, GQA_HD, GQA_HD) * vec_ref[1:2, V_GGK:V_GGK + LANES]).astype(BF16)
    vg_ref[0] = narrow(C_GV, LANES).astype(BF16)
    ggq = vec_ref[1:2, V_GGQ:V_GGQ + LANES]
    for p in range(GQA_HEADS // 2):
        xs = narrow(C_GQ + p * LANES, LANES)
        n = gqa_rope(xs * seg_inv(xs, GQA_HD, GQA_HD) * ggq)
        qg_ref[0, p] = (n * (GQA_HD ** -0.5 * LOG2E)).astype(BF16)

    y_ref[0] = proj(C_CA, CONV_CH) * jax.nn.sigmoid(proj(C_CG, CONV_CH))
    for j in range(N_BRANCH):
        gate_ref[0, :, j * d_model:(j + 1) * d_model] = jax.nn.sigmoid(
            proj(C_GATE + j * d_model, d_model)).astype(BF16)


def _pre_mixer(x, sc, sh, vec, w_in_r, w_uq_r, w_ukv_r, tables, tm):
    b, n, d = x.shape
    use_rope = tables is not None
    nt = n // tm
    bmap = lambda i, j: (j, 0, 0)
    in_specs = [pl.BlockSpec((1, tm, d), lambda i, j: (j, i, 0)),
                pl.BlockSpec((1, 1, d), bmap),
                pl.BlockSpec((1, 1, d), bmap),
                _const_spec(vec.shape), _const_spec(w_in_r.shape), _const_spec(w_uq_r.shape),
                _const_spec(w_ukv_r.shape)]
    args = [x, sc, sh, vec, w_in_r, w_uq_r, w_ukv_r]
    if use_rope:
        in_specs += [pl.BlockSpec((tm, LANES), lambda i, j: (i, 0))] * 6
        args += list(tables)
    hspec = lambda nh: pl.BlockSpec((1, nh, tm, LANES), lambda i, j: (j, 0, i, 0))
    tspec = lambda w: pl.BlockSpec((1, tm, w), lambda i, j: (j, i, 0))
    out_specs = [hspec(MLA_HEADS), hspec(MLA_HEADS), hspec(MLA_HEADS // 2), hspec(GQA_HEADS // 2),
                 tspec(LANES), tspec(LANES), tspec(CONV_CH), tspec(N_BRANCH * d)]
    sds = jax.ShapeDtypeStruct
    out_shape = [sds((b, MLA_HEADS, n, LANES), BF16), sds((b, MLA_HEADS, n, LANES), BF16),
                 sds((b, MLA_HEADS // 2, n, LANES), BF16), sds((b, GQA_HEADS // 2, n, LANES), BF16),
                 sds((b, n, LANES), BF16), sds((b, n, LANES), BF16),
                 sds((b, n, CONV_CH), F32), sds((b, n, N_BRANCH * d), BF16)]
    return pl.pallas_call(
        functools.partial(_pre_mixer_kernel, use_rope=use_rope, d_model=d),
        grid=(nt, b), in_specs=in_specs, out_specs=out_specs, out_shape=out_shape,
        compiler_params=_params("parallel", "parallel"),
        name="pre_mixer_rope" if use_rope else "pre_mixer",
    )(*args)


def _attn_kernel(q_ref, *refs, n_seg, head_map, pairs):
    kv_refs = refs[:2 * n_seg]
    o_ref = refs[2 * n_seg]
    lane = lax.broadcasted_iota(jnp.int32, (1, LANES), 1)
    lo = lane < HALF

    def one_head(hh):
        qidx, q_half, kidx, vidx = head_map[hh]
        q = q_ref[0, qidx]
        if q_half is not None:
            q = jnp.where(lo if q_half == 0 else jnp.logical_not(lo), q, jnp.zeros_like(q))
        ss =[lax.dot_general(q, kv_refs[2 * s][0, kidx], NT_DIMS, preferred_element_type=F32)
              for s in range(n_seg)]
        m = functools.reduce(jnp.maximum, [jnp.max(s, axis=-1, keepdims=True) for s in ss])
        ps = [jnp.exp2(s - m) for s in ss]
        l = functools.reduce(jnp.add, [jnp.sum(p, axis=-1, keepdims=True) for p in ps])
        o = functools.reduce(jnp.add, [_dot(ps[s].astype(BF16), kv_refs[2 * s + 1][0, vidx])
                                       for s in range(n_seg)])
        return o / l

    for p, (h_lo, h_hi) in enumerate(pairs):
        o_ref[0, :, p * LANES:(p + 1) * LANES] = jnp.where(lo, one_head(h_lo), one_head(h_hi)).astype(BF16)


MLA_HEAD_MAP = tuple((h, None, h, h // 2) for h in range(MLA_HEADS))
MLA_PAIRS = tuple((2 * p, 2 * p + 1) for p in range(MLA_HEADS // 2))
GQA_HEAD_MAP = tuple((h % (GQA_HEADS // 2), h // (GQA_HEADS // 2), 0, 0) for h in range(GQA_HEADS))
GQA_PAIRS = tuple((a, a + GQA_HEADS // 2) for a in range(GQA_HEADS // 2))


def _attention(q, kvs, head_map, pairs, tq, name):
    b, nh, n, _ = q.shape
    in_specs = [pl.BlockSpec((1, nh, tq, LANES), lambda i, j: (i, 0, j, 0))]
    args = [q]
    for a in kvs:
        in_specs.append(pl.BlockSpec((1,) + a.shape[1:], lambda i, j: (i, 0, 0, 0)))
        args.append(a)
    width = len(pairs) * LANES
    return pl.pallas_call(
        functools.partial(_attn_kernel, n_seg=len(kvs) // 2, head_map=head_map, pairs=pairs),
        grid=(b, n // tq), in_specs=in_specs,
        out_specs=pl.BlockSpec((1, tq, width), lambda i, j: (i, j, 0)),
        out_shape=jax.ShapeDtypeStruct((b, n, width), BF16),
        compiler_params=_params("parallel", "parallel"),
        name=name,
    )(*args)


CONV_PAD = 16
CONV_TAIL = 24
CONV_CHUNK = 128


def _conv_kernel(y_ref, cw_ref, vec_ref, z_ref, buf_ref, ph_ref, *, n):
    ch = y_ref.shape[-1]
    buf_ref[0:CONV_PAD, :] = jnp.zeros((CONV_PAD, ch), F32)
    buf_ref[CONV_PAD + n:CONV_PAD + n + CONV_TAIL, :] = jnp.zeros((CONV_TAIL, ch), F32)
    buf_ref[CONV_PAD:CONV_PAD + n, :] = y_ref[0]
    first = CONV_PAD - CONV_W // 2
    win_rows = CONV_CHUNK + CONV_W + 1

    def chunk(c, carry):
        r0 = pl.multiple_of(c * CONV_CHUNK, CONV_CHUNK)
        win = buf_ref[pl.ds(r0, win_rows), :]
        for s in range(SUBLANES):
            ph_ref[s] = win[s:s + ph_ref.shape[1], :]
        acc = jnp.zeros((CONV_CHUNK, ch), F32) + vec_ref[0:1, :]
        for k in range(CONV_W):
            a, s = divmod(first + k, SUBLANES)
            acc = acc + ph_ref[s, a * SUBLANES:a * SUBLANES + CONV_CHUNK, :] * cw_ref[k:k + 1, :]
        mu = jnp.mean(acc, axis=-1, keepdims=True)
        cen = acc - mu
        var = jnp.mean(cen * cen, axis=-1, keepdims=True)
        yn = cen * lax.rsqrt(var + EPS) * vec_ref[1:2, :] + vec_ref[2:3, :]
        z_ref[0, pl.ds(r0, CONV_CHUNK), :] = (yn * jax.nn.sigmoid(yn)).astype(BF16)
        return carry

    lax.fori_loop(0, n // CONV_CHUNK, chunk, 0)


def _conv(y, cw, cvec):
    b, n, ch = y.shape
    return pl.pallas_call(
        functools.partial(_conv_kernel, n=n),
        grid=(b,),
        in_specs=[pl.BlockSpec((1, n, ch), lambda i: (i, 0, 0)), _const_spec(cw.shape), _const_spec(cvec.shape)],
        out_specs=pl.BlockSpec((1, n, ch), lambda i: (i, 0, 0)),
        out_shape=jax.ShapeDtypeStruct((b, n, ch), BF16),
        scratch_shapes=[pltpu.VMEM((CONV_PAD + n + CONV_TAIL, ch), F32),
                        pltpu.VMEM((SUBLANES, CONV_CHUNK + CONV_TAIL, ch), F32)],
        compiler_params=_params("parallel"),
        name="conv",
    )(y, cw, cvec)


def _merge_kernel(x_ref, g_ref, om_ref, oc_ref, og_ref, gate_ref, wm_ref, wc_ref, wg_ref, wo_ref, o_ref, *, d_model):
    d = d_model
    merged = (gate_ref[0, :, 0:d].astype(F32) * _dot(om_ref[0], wm_ref[...])
              + gate_ref[0, :, d:2 * d].astype(F32) * _dot(oc_ref[0], wc_ref[...])
              + gate_ref[0, :, 2 * d:3 * d].astype(F32) * _dot(og_ref[0], wg_ref[...]))
    o_ref[0] = x_ref[0] + g_ref[0] * _dot(merged.astype(BF16), wo_ref[...])


def _merge(x, g1, o_mla, z, o_gqa, gates, wm, wc, wg, wo, tm):
    b, n, d = x.shape
    tok = lambda w: pl.BlockSpec((1, tm, w), lambda i, j: (i, j, 0))
    return pl.pallas_call(
        functools.partial(_merge_kernel, d_model=d),
        grid=(b, n // tm),
        in_specs=[tok(d), pl.BlockSpec((1, 1, d), lambda i, j: (i, 0, 0)), tok(o_mla.shape[-1]), tok(z.shape[-1]),
                  tok(o_gqa.shape[-1]), tok(N_BRANCH * d), _const_spec(wm.shape), _const_spec(wc.shape),
                  _const_spec(wg.shape), _const_spec(wo.shape)],
        out_specs=tok(d),
        out_shape=jax.ShapeDtypeStruct((b, n, d), F32),
        compiler_params=_params("parallel", "parallel"),
        name="merge",
    )(x, g1, o_mla, z, o_gqa, gates, wm, wc, wg, wo)


def _first_argmax4(v):
    m = functools.reduce(jnp.maximum, v)
    idx = jnp.where(v[0] == m, 0, jnp.where(v[1] == m, 1, jnp.where(v[2] == m, 2, 3)))
    return m, idx


def _pick4(idx, v):
    return jnp.where(idx == 0, v[0], jnp.where(idx == 1, v[1], jnp.where(idx == 2, v[2], v[3])))


def _moe_pre_kernel(x_ref, sc_ref, sh_ref, g_ref, rw_ref, rb_ref, cin_ref, hx_ref, bkt_ref, rank_ref, cnt_ref,
                    *, d_model):
    tm = x_ref.shape[1]
    first = (pl.program_id(0) == 0) & (pl.program_id(1) == 0)

    @pl.when(first)
    def _():
        cnt_ref[...] = cin_ref[...]

    x = x_ref[0]
    inv = lax.rsqrt(jnp.mean(x * x, axis=-1, keepdims=True) + EPS)
    h = (x * inv * g_ref[...]) * (1.0 + sc_ref[0]) + sh_ref[0]
    hx_ref[0, :, 0:d_model] = h

    hh, hl = _split_hi_lo(h)
    rwh, rwl = rw_ref[0], rw_ref[1]
    logits = (lax.dot_general(rwh, hh, NT_DIMS, preferred_element_type=F32)
              + lax.dot_general(rwh, hl, NT_DIMS, preferred_element_type=F32)
              + lax.dot_general(rwl, hh, NT_DIMS, preferred_element_type=F32))
    s_all = jax.nn.sigmoid(logits)
    sel_all = s_all + rb_ref[...]
    s = [s_all[e:e + 1, :] for e in range(N_EXPERTS)]
    sel = [sel_all[e:e + 1, :] for e in range(N_EXPERTS)]

    gscore = []
    for g in range(N_GROUPS):
        a, b, c, d = sel[4 * g:4 * g + 4]
        gscore.append(functools.reduce(jnp.maximum, [a + b, a + c, a + d, b + c, b + d, c + d]))
    _, gi = _first_argmax4(gscore)
    v = [_pick4(gi, [sel[4 * g + j] for g in range(N_GROUPS)]) for j in range(EXPERTS_PER_GROUP)]
    u = [_pick4(gi, [s[4 * g + j] for g in range(N_GROUPS)]) for j in range(EXPERTS_PER_GROUP)]
    _, i1 = _first_argmax4(v)
    _, i2 = _first_argmax4([jnp.where(i1 == j, -jnp.inf, v[j]) for j in range(EXPERTS_PER_GROUP)])
    e_lo = jnp.minimum(i1, i2)
    e_hi = jnp.maximum(i1, i2)
    pair = jnp.where(e_lo == 0, e_hi - 1, jnp.where(e_lo == 1, e_hi + 1, N_PAIRS - 1))
    bucket = gi * N_PAIRS + pair
    u_lo = _pick4(e_lo, u)
    u_hi = _pick4(e_hi, u)
    den = u_lo + u_hi
    wts = jnp.concatenate([u_lo / den, u_hi / den, jnp.zeros((LANES - 2, tm), F32)], axis=0)
    hx_ref[0, :, d_model:d_model + LANES] = wts.T

    onehot = lax.broadcasted_iota(jnp.int32, (BUCKET_ROWS, tm), 0) == bucket
    oh = jnp.where(onehot, 1.0, 0.0)
    before = lax.broadcasted_iota(jnp.int32, (tm, tm), 0) < lax.broadcasted_iota(jnp.int32, (tm, tm), 1)
    cum = _dot(oh.astype(BF16), jnp.where(before, 1.0, 0.0).astype(BF16))
    rank = jnp.sum(oh * (cum + cnt_ref[:, 0:1]), axis=0, keepdims=True)
    bkt_ref[0, 0] = bucket
    rank_ref[0, 0] = rank.astype(jnp.int32)
    cnt_ref[...] = cnt_ref[...] + jnp.sum(oh, axis=1, keepdims=True)


def _moe_pre(x, sc, sh, g, rw, rb, cin, tm):
    b, n, d = x.shape
    nt = n // tm
    sds = jax.ShapeDtypeStruct
    ispec = pl.BlockSpec((1, 1, 1, tm), lambda i, j: (i, j, 0, 0))
    bmap = lambda i, j: (i, 0, 0)
    return pl.pallas_call(
        functools.partial(_moe_pre_kernel, d_model=d),
        grid=(b, nt),
        in_specs=[pl.BlockSpec((1, tm, d), lambda i, j: (i, j, 0)), pl.BlockSpec((1, 1, d), bmap),
                  pl.BlockSpec((1, 1, d), bmap), _const_spec(g.shape), _const_spec(rw.shape),
                  _const_spec(rb.shape), _const_spec(cin.shape)],
        out_specs=[pl.BlockSpec((1, tm, d + LANES), lambda i, j: (i, j, 0)), ispec, ispec,
                   _const_spec((BUCKET_ROWS, LANES))],
        out_shape=[sds((b, n, d + LANES), F32), sds((b, nt, 1, tm), jnp.int32), sds((b, nt, 1, tm), jnp.int32),
                   sds((BUCKET_ROWS, LANES), F32)],
        compiler_params=_params("arbitrary", "arbitrary"),
        name="moe_pre",
    )(x, sc, sh, g, rw, rb, cin)


ROW_UNROLL = 4
N_DMA_PRIORITIES = 2


def _row_copies(tm, row_copy, pos_ref):
    def issue(i, c):
        for q in range(N_DMA_PRIORITIES):
            r = i * N_DMA_PRIORITIES + q
            row_copy(r, pos_ref[0, 0, r]).start(priority=q)
        return c

    def drain(r, c):
        row_copy(0, 0).wait()
        return c

    lax.fori_loop(0, tm // N_DMA_PRIORITIES, issue, 0, unroll=ROW_UNROLL)
    lax.fori_loop(0, tm, drain, 0, unroll=2 * ROW_UNROLL)


def _scatter_kernel(pos_ref, src_ref, hs_in_ref, hs_ref, sem):
    del hs_in_ref

    def row_copy(r, p):
        return pltpu.make_async_copy(src_ref.at[pl.ds(r, 1)], hs_ref.at[pl.ds(p, 1)], sem)

    _row_copies(src_ref.shape[0], row_copy, pos_ref)


def _scatter(pos3, src, hs, tm):
    t, w = src.shape
    return pl.pallas_call(
        _scatter_kernel,
        grid=(t // tm,),
        in_specs=[pl.BlockSpec((1, 1, tm), lambda i: (i, 0, 0), memory_space=pltpu.SMEM),
                  pl.BlockSpec((tm, w), lambda i: (i, 0)),
                  pl.BlockSpec(memory_space=pl.ANY)],
        out_specs=pl.BlockSpec(memory_space=pl.ANY),
        out_shape=jax.ShapeDtypeStruct(hs.shape, hs.dtype),
        scratch_shapes=[pltpu.SemaphoreType.DMA(())],
        input_output_aliases={2: 0},
        compiler_params=_params("arbitrary"),
        name="scatter_rows",
    )(pos3, src, hs)


def _gather_res_kernel(pos_ref, x_ref, g_ref, ys_ref, o_ref, buf_ref, sem):
    def row_copy(r, p):
        return pltpu.make_async_copy(ys_ref.at[pl.ds(p, 1)], buf_ref.at[pl.ds(r, 1)], sem)

    _row_copies(x_ref.shape[1], row_copy, pos_ref)
    o_ref[0] = x_ref[0] + g_ref[0] * buf_ref[...]


def _gather_res(pos3, x, g2, ys, tm):
    b, n, d = x.shape
    nt = n // tm
    return pl.pallas_call(
        _gather_res_kernel,
        grid=(b, nt),
        in_specs=[pl.BlockSpec((1, 1, tm), lambda i, j: (i * nt + j, 0, 0), memory_space=pltpu.SMEM),
                  pl.BlockSpec((1, tm, d), lambda i, j: (i, j, 0)),
                  pl.BlockSpec((1, 1, d), lambda i, j: (i, 0, 0)),
                  pl.BlockSpec(memory_space=pl.ANY)],
        out_specs=pl.BlockSpec((1, tm, d), lambda i, j: (i, j, 0)),
        out_shape=jax.ShapeDtypeStruct((b, n, d), F32),
        scratch_shapes=[pltpu.VMEM((tm, d), F32), pltpu.SemaphoreType.DMA(())],
        compiler_params=_params("arbitrary", "arbitrary"),
        name="gather_residual",
    )(pos3, x, g2, ys)


def _experts_kernel(e1_ref, e2_ref, blk_ref, valid_ref, hs_ref, wg1_ref, wu1_ref, wd1_ref, wg2_ref, wu2_ref,
                    wd2_ref, o_ref, *, d_model):
    valid = valid_ref[pl.program_id(0)] > 0

    @pl.when(jnp.logical_not(valid))
    def _():
        o_ref[...] = jnp.zeros(o_ref.shape, F32)

    @pl.when(valid)
    def _():
        xb = hs_ref[:, 0:d_model].astype(BF16)

        def half(wg_ref, wu_ref, w):
            gte = _dot(xb, wg_ref[0])
            return (gte * jax.nn.sigmoid(gte) * _dot(xb, wu_ref[0]) * w).astype(BF16)

        h1 = half(wg1_ref, wu1_ref, hs_ref[:, d_model:d_model + 1])
        h2 = half(wg2_ref, wu2_ref, hs_ref[:, d_model + 1:d_model + 2])
        o_ref[...] = _dot(h1, wd1_ref[0]) + _dot(h2, wd2_ref[0])


def _experts(e1, e2, blk, valid, hs, wg, wu, wd, tm):
    tp, w = hs.shape
    d = w - LANES
    f = wg.shape[-1]
    up1 = pl.BlockSpec((1, d, f), lambda i, e1, e2, blk, valid: (e1[i], 0, 0))
    up2 = pl.BlockSpec((1, d, f), lambda i, e1, e2, blk, valid: (e2[i], 0, 0))
    dn1 = pl.BlockSpec((1, f, d), lambda i, e1, e2, blk, valid: (e1[i], 0, 0))
    dn2 = pl.BlockSpec((1, f, d), lambda i, e1, e2, blk, valid: (e2[i], 0, 0))
    grid_spec = pltpu.PrefetchScalarGridSpec(
        num_scalar_prefetch=4, grid=(tp // tm,),
        in_specs=[pl.BlockSpec((tm, w), lambda i, e1, e2, blk, valid: (blk[i], 0)), up1, up1, dn1, up2, up2, dn2],
        out_specs=pl.BlockSpec((tm, d), lambda i, e1, e2, blk, valid: (i, 0)))
    return pl.pallas_call(
        functools.partial(_experts_kernel, d_model=d),
        grid_spec=grid_spec,
        out_shape=jax.ShapeDtypeStruct((tp, d), F32),
        compiler_params=_params("arbitrary"),
        name="moe_experts",
    )(e1, e2, blk, valid, hs, wg, wu, wd, wg, wu, wd)


PAIR_LO = np.array([0, 0, 0, 1, 1, 2], np.int32)
PAIR_HI = np.array([1, 2, 3, 2, 3, 3], np.int32)


def _moe_plan(counts, n_tiles, tm):
    c = counts[:N_BUCKETS]
    nt = (c + tm - 1) // tm
    tile_end = jnp.cumsum(nt)
    row_off = (tile_end - nt) * tm
    n_valid = tile_end[-1]
    i = jnp.arange(n_tiles, dtype=jnp.int32)
    blk = jnp.minimum(i, n_valid - 1)
    bkt = jnp.minimum(jnp.searchsorted(tile_end, blk, side="right"), N_BUCKETS - 1).astype(jnp.int32)
    grp = bkt // N_PAIRS
    e1 = grp * EXPERTS_PER_GROUP + jnp.asarray(PAIR_LO)[bkt % N_PAIRS]
    e2 = grp * EXPERTS_PER_GROUP + jnp.asarray(PAIR_HI)[bkt % N_PAIRS]
    return row_off, e1, e2, blk.astype(jnp.int32), (i < n_valid).astype(jnp.int32)


def _moe(parts, rw, rb, wg, wu, wd, tm_exp):
    cin = jnp.zeros((BUCKET_ROWS, LANES), F32)
    pre = []
    for x, sc, sh, g2, gain, tm in parts:
        hx, bkt, rank, cin = _moe_pre(x, sc, sh, gain, rw, rb, cin, tm)
        pre.append((hx, bkt, rank))
    total = sum(p[0].shape[0] * p[0].shape[1] for p in parts)
    n_tiles = -(-total // tm_exp) + N_BUCKETS
    counts = cin[:, 0].astype(jnp.int32)
    row_off, e1, e2, blk, valid = _moe_plan(counts, n_tiles, tm_exp)
    w = pre[0][0].shape[-1]
    hs = jnp.zeros((n_tiles * tm_exp, w), F32)
    poss = []
    for (x, sc, sh, g2, gain, tm), (hx, bkt, rank) in zip(parts, pre):
        start = sum(jnp.where(bkt == k, row_off[k], 0) for k in range(N_BUCKETS))
        pos = (start + rank).reshape(-1, 1, tm)
        poss.append(pos)
        hs = _scatter(pos, hx.reshape(-1, w), hs, tm)
    ys = _experts(e1, e2, blk, valid, hs, wg, wu, wd, tm_exp)
    return [_gather_res(pos, x, g2, ys, tm) for (x, sc, sh, g2, gain, tm), pos in zip(parts, poss)]


def _relayout_w_in(w_in, d_model):
    sizes = (MLA_Q_RANK, MLA_KV_RANK, MLA_ROPE, 2 * CONV_CH, GQA_HEADS * GQA_HD, GQA_KV_HEADS * GQA_HD,
             GQA_KV_HEADS * GQA_HD, N_BRANCH * d_model)
    offs = np.cumsum((0,) + sizes)
    cq, ckv, kpe, conv, gq, gk, gv, gates = [w_in[..., offs[i]:offs[i + 1]] for i in range(len(sizes))]
    lead = w_in.shape[:-1]
    z = lambda n: jnp.zeros(lead + (n,), w_in.dtype)
    head = lambda hh: gq[..., hh * GQA_HD:(hh + 1) * GQA_HD]
    gq_slabs = [head(hh) for p in range(GQA_HEADS // 2) for hh in (p, p + GQA_HEADS // 2)]
    out = jnp.concatenate([cq, ckv, z(HALF), kpe, z(HALF - MLA_ROPE), gk, gv] + gq_slabs + [conv, gates], axis=-1)
    return out.astype(BF16)


def _relayout_uq(w_uq):
    l, r, _ = w_uq.shape
    w = w_uq.reshape(l, r, MLA_HEADS, MLA_QK)
    w = jnp.pad(w, ((0, 0), (0, 0), (0, 0), (0, LANES - MLA_QK)))
    return w.reshape(l, r, MLA_HEADS * LANES).astype(BF16)


def _relayout_ukv(w_ukv):
    l, r, _ = w_ukv.shape
    w = w_ukv.reshape(l, r, MLA_HEADS, MLA_NOPE + MLA_V)
    k = jnp.pad(w[..., :MLA_NOPE], ((0, 0), (0, 0), (0, 0), (0, LANES - MLA_NOPE)))
    v = w[..., MLA_NOPE:]
    return jnp.concatenate([k.reshape(l, r, MLA_HEADS * LANES), v.reshape(l, r, MLA_HEADS * MLA_V)],
                           axis=-1).astype(BF16)


def _rope_tables(n, r):
    rows = n // GRID_W
    row = jnp.repeat(jnp.arange(rows, dtype=jnp.int32), GRID_W).astype(F32)
    col = jnp.tile(jnp.arange(GRID_W, dtype=jnp.int32), rows).astype(F32)
    dim = r // 2
    inv = ROPE_BASE ** (-jnp.arange(0, dim, 2, dtype=F32) / dim)
    ar, ac = row[:, None] * inv[None, :], col[:, None] * inv[None, :]
    zero = jnp.zeros_like(ar)
    cos = jnp.concatenate([jnp.cos(ar), jnp.cos(ar), jnp.cos(ac), jnp.cos(ac)], axis=-1)
    sa = jnp.concatenate([-jnp.sin(ar), zero, -jnp.sin(ac), zero], axis=-1)
    sb = jnp.concatenate([zero, jnp.sin(ar), zero, jnp.sin(ac)], axis=-1)
    return cos, sa, sb


def _all_tables(n):
    mc, msa, msb = _rope_tables(n, MLA_ROPE)
    one = jnp.ones((n, MLA_NOPE), F32)
    zero = jnp.zeros((n, MLA_NOPE), F32)
    pad1 = jnp.ones((n, LANES - MLA_QK), F32)
    pad0 = jnp.zeros((n, LANES - MLA_QK), F32)
    mla = (jnp.concatenate([one, mc, pad1], -1), jnp.concatenate([zero, msa, pad0], -1),
           jnp.concatenate([zero, msb, pad0], -1))
    gqa = tuple(jnp.concatenate([t, t], -1) for t in _rope_tables(n, GQA_HD))
    return mla + gqa


def kernel(x, c, ctx, c_ctx, w_mod, b_mod, norm1_g, norm2_g, w_in, mla_q_norm, mla_kv_norm, mla_w_uq, mla_w_ukv,
           mla_q_gain, mla_k_gain, mla_w_o, conv_w, conv_b, conv_ln_g, conv_ln_b, conv_w_o, gqa_q_gain, gqa_k_gain,
           gqa_w_o, w_out, router_w, router_bias, moe_w_gate, moe_w_up, moe_w_down):
    b, n, d = x.shape
    n_ctx = ctx.shape[1]
    depth = w_mod.shape[0]
    assert n % GRID_W == 0 and n % 512 == 0 and n_ctx % CONV_CHUNK == 0 and d % LANES == 0

    tm = 512
    tm_ctx = min(n_ctx, 256)
    tq = 512
    tm_exp = 512

    m_rows = -(-(b + 1) // 8) * 8
    c_all = jnp.concatenate([c, c_ctx[None, :], jnp.zeros((m_rows - b - 1, d), F32)], axis=0)
    mod = _modulation(c_all, w_mod, b_mod)

    w_in_r = _relayout_w_in(w_in, d)
    w_uq_r = _relayout_uq(mla_w_uq)
    w_ukv_r = _relayout_ukv(mla_w_ukv)
    zl = lambda k: jnp.zeros((depth, k), F32)
    qg128 = jnp.concatenate([mla_q_gain, zl(LANES - MLA_QK)], -1)
    kg128 = jnp.concatenate([mla_k_gain, zl(LANES - MLA_QK)], -1)
    row1 = jnp.concatenate([mla_q_norm, mla_kv_norm, qg128, kg128, gqa_q_gain, gqa_q_gain, gqa_k_gain, gqa_k_gain,
                            zl(d - V_GGK - LANES)], -1)
    vec = jnp.stack([norm1_g, row1], axis=1)
    gqa_perm = np.concatenate([np.arange(GQA_HD) + GQA_HD * hh for pair in GQA_PAIRS for hh in pair])
    wm_o = mla_w_o.astype(BF16)
    wg_o = gqa_w_o[:, gqa_perm, :].astype(BF16)
    wc_o = conv_w_o.astype(BF16)
    wo = w_out.astype(BF16)
    cw = jnp.pad(conv_w, ((0, 0), (0, 32 - CONV_W), (0, 0)))
    cvec = jnp.pad(jnp.stack([conv_b, conv_ln_g, conv_ln_b], axis=1), ((0, 0), (0, 5), (0, 0)))
    rwt = router_w.T
    rw_hi = rwt.astype(BF16)
    rw = jnp.stack([rw_hi, (rwt - rw_hi.astype(F32)).astype(BF16)])
    rb = router_bias.reshape(N_EXPERTS, 1)
    wg_e, wu_e, wd_e = moe_w_gate.astype(BF16), moe_w_up.astype(BF16), moe_w_down.astype(BF16)
    tables = _all_tables(n)

    xc = ctx
    for l in range(depth):
        last = l == depth - 1
        sh1, sc1, g1, sh2, sc2, g2 = [t[:, None, :] for t in jnp.split(mod[l, :b], 6, axis=-1)]
        csh1, csc1, cg1, csh2, csc2, cg2 = [jnp.broadcast_to(t[None, :, :], (b, 1, d))
                                            for t in jnp.split(mod[l, b:b + 1], 6, axis=-1)]
        qm, km, vm, qg, kg, vg, y, gates = _pre_mixer(x, sc1, sh1, vec[l], w_in_r[l], w_uq_r[l], w_ukv_r[l],
                                                      tables, tm)
        cqm, ckm, cvm, cqg, ckg, cvg, cy, cgates = _pre_mixer(xc, csc1, csh1, vec[l], w_in_r[l], w_uq_r[l],
                                                              w_ukv_r[l], None, tm_ctx)
        kg4, vg4, ckg4, cvg4 = kg[:, None], vg[:, None], ckg[:, None], cvg[:, None]
        o_mla = _attention(qm, [km, vm, ckm, cvm], MLA_HEAD_MAP, MLA_PAIRS, tq, "attn_mla")
        o_gqa = _attention(qg, [kg4, vg4, ckg4, cvg4], GQA_HEAD_MAP, GQA_PAIRS, tq, "attn_gqa")
        z = _conv(y, cw[l], cvec[l])
        x = _merge(x, g1, o_mla, z, o_gqa, gates, wm_o[l], wc_o[l], wg_o[l], wo[l], tm)
        moe_parts = [(x, sc2, sh2, g2, norm2_g[l][None, :], tm)]
        if not last:
            co_mla = _attention(cqm, [ckm, cvm], MLA_HEAD_MAP, MLA_PAIRS, tm_ctx, "attn_mla_ctx")
            co_gqa = _attention(cqg, [ckg4, cvg4], GQA_HEAD_MAP, GQA_PAIRS, tm_ctx, "attn_gqa_ctx")
            cz = _conv(cy, cw[l], cvec[l])
            xc = _merge(xc, cg1, co_mla, cz, co_gqa, cgates, wm_o[l], wc_o[l], wg_o[l], wo[l], tm_ctx)
            moe_parts.append((xc, csc2, csh2, cg2, norm2_g[l][None, :], tm_ctx))
        outs = _moe(moe_parts, rw, rb, wg_e[l], wu_e[l], wd_e[l], tm_exp)
        x = outs[0]
        if not last:
            xc = outs[1]
    return x
```

```python
import functools
import math

import numpy as np
import jax
import jax.numpy as jnp
from jax import lax
from jax.experimental import pallas as pl
from jax.experimental.pallas import tpu as pltpu

GRID_W = 64
ROPE_BASE = 10000.0
EPS = 1e-6

MLA_HEADS = 8
MLA_Q_RANK = 256
MLA_KV_RANK = 128
MLA_NOPE = 64
MLA_ROPE = 32
MLA_V = 64
MLA_QK = MLA_NOPE + MLA_ROPE
CONV_CH = 512
CONV_W = 31
GQA_HEADS = 8
GQA_KV_HEADS = 2
GQA_HD = 64
N_BRANCH = 3
N_EXPERTS = 16
N_GROUPS = 4
EXPERTS_PER_GROUP = 4
D_EXPERT = 512

LANES = 128
SUBLANES = 8
HALF = LANES // 2
N_PAIRS = 6
N_BUCKETS = N_GROUPS * N_PAIRS
BUCKET_ROWS = 32
LOG2E = math.log2(math.e)
VMEM_LIMIT = 56 * 1024 * 1024

C_CQ = 0
C_CKV = C_CQ + MLA_Q_RANK
C_KPE = C_CKV + MLA_KV_RANK
C_GK = C_KPE + LANES
C_GV = C_GK + LANES
C_GQ = C_GV + LANES
C_CA = C_GQ + GQA_HEADS * GQA_HD
C_CG = C_CA + CONV_CH
C_GATE = C_CG + CONV_CH
D_IN_R = C_GATE

V_QN = 0
V_KVN = 256
V_QG = 384
V_KG = 512
V_GGQ = 640
V_GGK = 768

BF16 = jnp.bfloat16
F32 = jnp.float32
NT_DIMS = (((1,), (1,)), ((), ()))


def _dot(a, b):
    return jnp.dot(a, b, preferred_element_type=F32)


def _sigmoid(v):
    return 0.5 * jnp.tanh(0.5 * v) + 0.5


def _split_hi_lo(a):
    hi = a.astype(BF16)
    lo = (a - hi.astype(F32)).astype(BF16)
    return hi, lo


def _params(*sem):
    return pltpu.CompilerParams(dimension_semantics=sem, vmem_limit_bytes=VMEM_LIMIT)


def _const_spec(shape):
    nd = len(shape)
    return pl.BlockSpec(shape, lambda *_: (0,) * nd)


def _mod_kernel(c_ref, w_ref, b_ref, o_ref):
    c = c_ref[...]
    s = c * _sigmoid(c)
    sh, sl = _split_hi_lo(s)
    wh, wl = _split_hi_lo(w_ref[0])
    o_ref[0] = _dot(sh, wh) + _dot(sh, wl) + _dot(sl, wh) + b_ref[0]


def _modulation(c_all, w_mod, b_mod):
    depth, d, d6 = w_mod.shape
    m = c_all.shape[0]
    tn = 1536
    return pl.pallas_call(
        _mod_kernel,
        grid=(depth, d6 // tn),
        in_specs=[pl.BlockSpec((m, d), lambda l, j: (0, 0)),
                  pl.BlockSpec((1, d, tn), lambda l, j: (l, 0, j)),
                  pl.BlockSpec((1, 1, tn), lambda l, j: (l, 0, j))],
        out_specs=pl.BlockSpec((1, m, tn), lambda l, j: (l, 0, j)),
        out_shape=jax.ShapeDtypeStruct((depth, m, d6), F32),
        compiler_params=_params("parallel", "parallel"),
        name="modulation",
    )(c_all, w_mod, b_mod.reshape(depth, 1, d6))


def _rope(n, cos, sa, sb, quarter):
    return n * cos + pltpu.roll(n, LANES - quarter, 1) * sa + pltpu.roll(n, quarter, 1) * sb


def _pre_mixer_kernel(x_ref, sc_ref, sh_ref, vec_ref, w_ref, wuq_ref, wukv_ref, *rest, use_rope, d_model, t_sub):
    for r0 in range(0, x_ref.shape[1], t_sub):
        _pre_mixer_rows(slice(r0, r0 + t_sub), x_ref, sc_ref, sh_ref, vec_ref, w_ref, wuq_ref, wukv_ref, rest,
                        use_rope, d_model)


def _pre_mixer_rows(rows, x_ref, sc_ref, sh_ref, vec_ref, w_ref, wuq_ref, wukv_ref, rest, use_rope, d_model):
    if use_rope:
        mc_ref, msa_ref, msb_ref, gc_ref, gsa_ref, gsb_ref = rest[:6]
        rest = rest[6:]
    qm_ref, km_ref, vm_ref, qg_ref, kg_ref, vg_ref, y_ref, gate_ref = rest

    x = x_ref[0, rows, :]
    inv = lax.rsqrt(jnp.mean(x * x, axis=-1, keepdims=True) + EPS)
    h = (x * inv * vec_ref[0:1, :]) * (1.0 + sc_ref[0]) + sh_ref[0]
    hb = h.astype(BF16)

    lane = lax.broadcasted_iota(jnp.int32, (1, LANES), 1)
    lo = lane < HALF

    def proj(c0, width):
        return _dot(hb, w_ref[:, c0:c0 + width])

    def rms(v, g):
        return v * lax.rsqrt(jnp.mean(v * v, axis=-1, keepdims=True) + EPS) * g

    def seg_inv(xs, n_lo, n_hi):
        sq = xs * xs
        if n_hi is None:
            return lax.rsqrt(jnp.sum(sq, axis=-1, keepdims=True) * (1.0 / n_lo) + EPS)
        s_lo = jnp.sum(jnp.where(lo, sq, 0.0), axis=-1, keepdims=True)
        s_hi = jnp.sum(jnp.where(lo, 0.0, sq), axis=-1, keepdims=True)
        return jnp.where(lo, lax.rsqrt(s_lo * (1.0 / n_lo) + EPS), lax.rsqrt(s_hi * (1.0 / n_hi) + EPS))

    def mla_rope(n):
        return _rope(n, mc_ref[rows, :], msa_ref[rows, :], msb_ref[rows, :], MLA_ROPE // 4) if use_rope else n

    def gqa_rope(n):
        return _rope(n, gc_ref[rows, :], gsa_ref[rows, :], gsb_ref[rows, :], GQA_HD // 4) if use_rope else n

    small = proj(0, C_CA)

    def narrow(c0, width):
        return small[:, c0:c0 + width]

    cqn = rms(narrow(C_CQ, MLA_Q_RANK), vec_ref[1:2, V_QN:V_QN + MLA_Q_RANK])
    q = _dot(cqn.astype(BF16), wuq_ref[...])
    qgain = vec_ref[1:2, V_QG:V_QG + LANES]
    for hh in range(MLA_HEADS):
        xs = q[:, hh * LANES:(hh + 1) * LANES]
        n = mla_rope(xs * seg_inv(xs, MLA_NOPE, MLA_ROPE) * qgain)
        qm_ref[0, hh, rows, :] = (n * (MLA_QK ** -0.5 * LOG2E)).astype(BF16)

    kgain = vec_ref[1:2, V_KG:V_KG + LANES]
    kpe = narrow(C_KPE, LANES)
    kpe = mla_rope(kpe * seg_inv(kpe, MLA_ROPE, None) * kgain)
    ckvn = rms(narrow(C_CKV, MLA_KV_RANK), vec_ref[1:2, V_KVN:V_KVN + MLA_KV_RANK])
    kv = _dot(ckvn.astype(BF16), wukv_ref[...])
    for hh in range(MLA_HEADS):
        xs = kv[:, hh * LANES:(hh + 1) * LANES]
        km_ref[0, hh, rows, :] = (xs * seg_inv(xs, MLA_NOPE, None) * kgain + kpe).astype(BF16)
    v0 = MLA_HEADS * LANES
    for p in range(MLA_HEADS // 2):
        vm_ref[0, p, rows, :] = kv[:, v0 + p * LANES:v0 + (p + 1) * LANES].astype(BF16)

    gk = narrow(C_GK, LANES)
    ggk = vec_ref[1:2, V_GGK:V_GGK + LANES]
    kg_ref[0, rows, :] = gqa_rope(gk * seg_inv(gk, GQA_HD, GQA_HD) * ggk).astype(BF16)
    vg_ref[0, rows, :] = narrow(C_GV, LANES).astype(BF16)
    ggq = vec_ref[1:2, V_GGQ:V_GGQ + LANES]
    for p in range(GQA_HEADS // 2):
        xs = narrow(C_GQ + p * LANES, LANES)
        n = gqa_rope(xs * seg_inv(xs, GQA_HD, GQA_HD) * ggq)
        qg_ref[0, p, rows, :] = (n * (GQA_HD ** -0.5 * LOG2E)).astype(BF16)

    y_ref[0, rows, :] = proj(C_CA, CONV_CH) * _sigmoid(proj(C_CG, CONV_CH))
    for j in range(N_BRANCH):
        gate_ref[0, rows, j * d_model:(j + 1) * d_model] = _sigmoid(
            proj(C_GATE + j * d_model, d_model)).astype(BF16)


PRE_SUB = 512


def _pre_mixer(x, sc, sh, vec, w_in_r, w_uq_r, w_ukv_r, tables, tm):
    b, n, d = x.shape
    use_rope = tables is not None
    nt = n // tm
    bmap = lambda i, j: (j, 0, 0)
    in_specs = [pl.BlockSpec((1, tm, d), lambda i, j: (j, i, 0)),
                pl.BlockSpec((1, 1, d), bmap),
                pl.BlockSpec((1, 1, d), bmap),
                _const_spec(vec.shape), _const_spec(w_in_r.shape), _const_spec(w_uq_r.shape),
                _const_spec(w_ukv_r.shape)]
    args = [x, sc, sh, vec, w_in_r, w_uq_r, w_ukv_r]
    if use_rope:
        in_specs += [pl.BlockSpec((tm, LANES), lambda i, j: (i, 0))] * 6
        args += list(tables)
    hspec = lambda nh: pl.BlockSpec((1, nh, tm, LANES), lambda i, j: (j, 0, i, 0))
    tspec = lambda w: pl.BlockSpec((1, tm, w), lambda i, j: (j, i, 0))
    out_specs = [hspec(MLA_HEADS), hspec(MLA_HEADS), hspec(MLA_HEADS // 2), hspec(GQA_HEADS // 2),
                 tspec(LANES), tspec(LANES), tspec(CONV_CH), tspec(N_BRANCH * d)]
    sds = jax.ShapeDtypeStruct
    out_shape = [sds((b, MLA_HEADS, n, LANES), BF16), sds((b, MLA_HEADS, n, LANES), BF16),
                 sds((b, MLA_HEADS // 2, n, LANES), BF16), sds((b, GQA_HEADS // 2, n, LANES), BF16),
                 sds((b, n, LANES), BF16), sds((b, n, LANES), BF16),
                 sds((b, n, CONV_CH), F32), sds((b, n, N_BRANCH * d), BF16)]
    return pl.pallas_call(
        functools.partial(_pre_mixer_kernel, use_rope=use_rope, d_model=d, t_sub=min(tm, PRE_SUB)),
        grid=(nt, b), in_specs=in_specs, out_specs=out_specs, out_shape=out_shape,
        compiler_params=_params("parallel", "parallel"),
        name="pre_mixer_rope" if use_rope else "pre_mixer",
    )(*args)


def _attn_kernel(q_ref, *refs, n_seg, head_map, pairs, t_sub):
    kv_refs = refs[:2 * n_seg]
    o_ref = refs[2 * n_seg]
    lane = lax.broadcasted_iota(jnp.int32, (1, LANES), 1)
    lo = lane < HALF

    def one_head(hh, rows):
        qidx, q_half, kidx, vidx = head_map[hh]
        q = q_ref[0, qidx, rows, :]
        if q_half is not None:
            q = jnp.where(lo if q_half == 0 else jnp.logical_not(lo), q, jnp.zeros_like(q))
        ss =[lax.dot_general(q, kv_refs[2 * s][0, kidx], NT_DIMS, preferred_element_type=F32)
              for s in range(n_seg)]
        m = functools.reduce(jnp.maximum, [jnp.max(s, axis=-1, keepdims=True) for s in ss])
        ps = [jnp.exp2(s - m) for s in ss]
        l = functools.reduce(jnp.add, [jnp.sum(p, axis=-1, keepdims=True) for p in ps])
        o = functools.reduce(jnp.add, [_dot(ps[s].astype(BF16), kv_refs[2 * s + 1][0, vidx])
                                       for s in range(n_seg)])
        return o / l

    for r0 in range(0, q_ref.shape[2], t_sub):
        rows = slice(r0, r0 + t_sub)
        for p, (h_lo, h_hi) in enumerate(pairs):
            o_ref[0, rows, p * LANES:(p + 1) * LANES] = jnp.where(
                lo, one_head(h_lo, rows), one_head(h_hi, rows)).astype(BF16)


MLA_HEAD_MAP = tuple((h, None, h, h // 2) for h in range(MLA_HEADS))
MLA_PAIRS = tuple((2 * p, 2 * p + 1) for p in range(MLA_HEADS // 2))
GQA_HEAD_MAP = tuple((h % (GQA_HEADS // 2), h // (GQA_HEADS // 2), 0, 0) for h in range(GQA_HEADS))
GQA_PAIRS = tuple((a, a + GQA_HEADS // 2) for a in range(GQA_HEADS // 2))


ATTN_SUB = 512


def _attention(q, kvs, head_map, pairs, tq, name):
    b, nh, n, _ = q.shape
    in_specs = [pl.BlockSpec((1, nh, tq, LANES), lambda i, j: (i, 0, j, 0))]
    args = [q]
    for a in kvs:
        in_specs.append(pl.BlockSpec((1,) + a.shape[1:], lambda i, j: (i, 0, 0, 0)))
        args.append(a)
    width = len(pairs) * LANES
    return pl.pallas_call(
        functools.partial(_attn_kernel, n_seg=len(kvs) // 2, head_map=head_map, pairs=pairs,
                          t_sub=min(tq, ATTN_SUB)),
        grid=(b, n // tq), in_specs=in_specs,
        out_specs=pl.BlockSpec((1, tq, width), lambda i, j: (i, j, 0)),
        out_shape=jax.ShapeDtypeStruct((b, n, width), BF16),
        compiler_params=_params("parallel", "parallel"),
        name=name,
    )(*args)


CONV_PAD = 16
CONV_TAIL = 24
CONV_CHUNK = 128


def _conv_kernel(y_ref, cw_ref, vec_ref, z_ref, buf_ref, ph_ref, *, n):
    ch = y_ref.shape[-1]
    buf_ref[0:CONV_PAD, :] = jnp.zeros((CONV_PAD, ch), F32)
    buf_ref[CONV_PAD + n:CONV_PAD + n + CONV_TAIL, :] = jnp.zeros((CONV_TAIL, ch), F32)
    buf_ref[CONV_PAD:CONV_PAD + n, :] = y_ref[0]
    first = CONV_PAD - CONV_W // 2
    win_rows = CONV_CHUNK + CONV_W + 1

    def chunk(c, carry):
        r0 = pl.multiple_of(c * CONV_CHUNK, CONV_CHUNK)
        win = buf_ref[pl.ds(r0, win_rows), :]
        for s in range(SUBLANES):
            ph_ref[s] = win[s:s + ph_ref.shape[1], :]
        acc = jnp.zeros((CONV_CHUNK, ch), F32) + vec_ref[0:1, :]
        for k in range(CONV_W):
            a, s = divmod(first + k, SUBLANES)
            acc = acc + ph_ref[s, a * SUBLANES:a * SUBLANES + CONV_CHUNK, :] * cw_ref[k:k + 1, :]
        mu = jnp.mean(acc, axis=-1, keepdims=True)
        cen = acc - mu
        var = jnp.mean(cen * cen, axis=-1, keepdims=True)
        yn = cen * lax.rsqrt(var + EPS) * vec_ref[1:2, :] + vec_ref[2:3, :]
        z_ref[0, pl.ds(r0, CONV_CHUNK), :] = (yn * _sigmoid(yn)).astype(BF16)
        return carry

    lax.fori_loop(0, n // CONV_CHUNK, chunk, 0)


def _conv(y, cw, cvec):
    b, n, ch = y.shape
    return pl.pallas_call(
        functools.partial(_conv_kernel, n=n),
        grid=(b,),
        in_specs=[pl.BlockSpec((1, n, ch), lambda i: (i, 0, 0)), _const_spec(cw.shape), _const_spec(cvec.shape)],
        out_specs=pl.BlockSpec((1, n, ch), lambda i: (i, 0, 0)),
        out_shape=jax.ShapeDtypeStruct((b, n, ch), BF16),
        scratch_shapes=[pltpu.VMEM((CONV_PAD + n + CONV_TAIL, ch), F32),
                        pltpu.VMEM((SUBLANES, CONV_CHUNK + CONV_TAIL, ch), F32)],
        compiler_params=_params("parallel"),
        name="conv",
    )(y, cw, cvec)


def _merge_route_kernel(x_ref, g_ref, om_ref, oc_ref, og_ref, gate_ref, wm_ref, wc_ref, wg_ref, wo_ref,
                        sc_ref, sh_ref, n2_ref, rw_ref, rb_ref, cin_ref,
                        o_ref, hx_ref, bkt_ref, rank_ref, cnt_ref, *, d_model):
    d = d_model
    merged = (gate_ref[0, :, 0:d].astype(F32) * _dot(om_ref[0], wm_ref[...])
              + gate_ref[0, :, d:2 * d].astype(F32) * _dot(oc_ref[0], wc_ref[...])
              + gate_ref[0, :, 2 * d:3 * d].astype(F32) * _dot(og_ref[0], wg_ref[...]))
    x_new = x_ref[0] + g_ref[0] * _dot(merged.astype(BF16), wo_ref[...])
    o_ref[0] = x_new
    _route(x_new, sc_ref, sh_ref, n2_ref, rw_ref, rb_ref, cin_ref, hx_ref, bkt_ref, rank_ref, cnt_ref, d)


def _merge_route(x, g1, o_mla, z, o_gqa, gates, wm, wc, wg, wo, sc2, sh2, n2, rw, rb, cin, tm):
    b, n, d = x.shape
    nt = n // tm
    sds = jax.ShapeDtypeStruct
    tok = lambda w: pl.BlockSpec((1, tm, w), lambda i, j: (i, j, 0))
    per_b = pl.BlockSpec((1, 1, d), lambda i, j: (i, 0, 0))
    ispec = pl.BlockSpec((1, 1, 1, tm), lambda i, j: (i, j, 0, 0))
    return pl.pallas_call(
        functools.partial(_merge_route_kernel, d_model=d),
        grid=(b, nt),
        in_specs=[tok(d), per_b, tok(o_mla.shape[-1]), tok(z.shape[-1]), tok(o_gqa.shape[-1]), tok(N_BRANCH * d),
                  _const_spec(wm.shape), _const_spec(wc.shape), _const_spec(wg.shape), _const_spec(wo.shape),
                  per_b, per_b, _const_spec(n2.shape), _const_spec(rw.shape), _const_spec(rb.shape),
                  _const_spec(cin.shape)],
        out_specs=[tok(d), tok(d + LANES), ispec, ispec, _const_spec((BUCKET_ROWS, LANES))],
        out_shape=[sds((b, n, d), F32), sds((b, n, d + LANES), F32), sds((b, nt, 1, tm), jnp.int32),
                   sds((b, nt, 1, tm), jnp.int32), sds((BUCKET_ROWS, LANES), F32)],
        compiler_params=_params("arbitrary", "arbitrary"),
        name="merge_route",
    )(x, g1, o_mla, z, o_gqa, gates, wm, wc, wg, wo, sc2, sh2, n2, rw, rb, cin)


def _first_argmax4(v):
    m = functools.reduce(jnp.maximum, v)
    idx = jnp.where(v[0] == m, 0, jnp.where(v[1] == m, 1, jnp.where(v[2] == m, 2, 3)))
    return m, idx


def _pick4(idx, v):
    return jnp.where(idx == 0, v[0], jnp.where(idx == 1, v[1], jnp.where(idx == 2, v[2], v[3])))


def _route(x, sc_ref, sh_ref, g_ref, rw_ref, rb_ref, cin_ref, hx_ref, bkt_ref, rank_ref, cnt_ref, d_model):
    tm = x.shape[0]
    first = (pl.program_id(0) == 0) & (pl.program_id(1) == 0)

    @pl.when(first)
    def _():
        cnt_ref[...] = cin_ref[...]

    inv = lax.rsqrt(jnp.mean(x * x, axis=-1, keepdims=True) + EPS)
    h = (x * inv * g_ref[...]) * (1.0 + sc_ref[0]) + sh_ref[0]
    hx_ref[0, :, 0:d_model] = h

    hh, hl = _split_hi_lo(h)
    rwh, rwl = rw_ref[0], rw_ref[1]
    logits = (lax.dot_general(rwh, hh, NT_DIMS, preferred_element_type=F32)
              + lax.dot_general(rwh, hl, NT_DIMS, preferred_element_type=F32)
              + lax.dot_general(rwl, hh, NT_DIMS, preferred_element_type=F32))
    s_all = _sigmoid(logits)
    sel_all = s_all + rb_ref[...]
    s = [s_all[e:e + 1, :] for e in range(N_EXPERTS)]
    sel = [sel_all[e:e + 1, :] for e in range(N_EXPERTS)]

    gscore = []
    for g in range(N_GROUPS):
        a, b, c, d = sel[4 * g:4 * g + 4]
        gscore.append(functools.reduce(jnp.maximum, [a + b, a + c, a + d, b + c, b + d, c + d]))
    _, gi = _first_argmax4(gscore)
    v = [_pick4(gi, [sel[4 * g + j] for g in range(N_GROUPS)]) for j in range(EXPERTS_PER_GROUP)]
    u = [_pick4(gi, [s[4 * g + j] for g in range(N_GROUPS)]) for j in range(EXPERTS_PER_GROUP)]
    _, i1 = _first_argmax4(v)
    _, i2 = _first_argmax4([jnp.where(i1 == j, -jnp.inf, v[j]) for j in range(EXPERTS_PER_GROUP)])
    e_lo = jnp.minimum(i1, i2)
    e_hi = jnp.maximum(i1, i2)
    pair = jnp.where(e_lo == 0, e_hi - 1, jnp.where(e_lo == 1, e_hi + 1, N_PAIRS - 1))
    bucket = gi * N_PAIRS + pair
    u_lo = _pick4(e_lo, u)
    u_hi = _pick4(e_hi, u)
    den = u_lo + u_hi
    wts = jnp.concatenate([u_lo / den, u_hi / den, jnp.zeros((LANES - 2, tm), F32)], axis=0)
    hx_ref[0, :, d_model:d_model + LANES] = wts.T

    onehot = lax.broadcasted_iota(jnp.int32, (BUCKET_ROWS, tm), 0) == bucket
    oh = jnp.where(onehot, 1.0, 0.0)
    before = lax.broadcasted_iota(jnp.int32, (tm, tm), 0) < lax.broadcasted_iota(jnp.int32, (tm, tm), 1)
    cum = _dot(oh.astype(BF16), jnp.where(before, 1.0, 0.0).astype(BF16))
    rank = jnp.sum(oh * (cum + cnt_ref[:, 0:1]), axis=0, keepdims=True)
    bkt_ref[0, 0] = bucket
    rank_ref[0, 0] = rank.astype(jnp.int32)
    cnt_ref[...] = cnt_ref[...] + jnp.sum(oh, axis=1, keepdims=True)


ROW_UNROLL = 2
N_DMA_PRIORITIES = 2


def _row_copies(tm, row_copy, hi_ref, lo_ref):
    def issue(i, c):
        for j in range(SUBLANES):
            r = i * SUBLANES + j
            lo = lo_ref[0, 0, r] & (SUBLANES - 1)
            row_copy(i, j, hi_ref[0, 0, r], lo).start(priority=j % N_DMA_PRIORITIES)
        return c

    def drain(r, c):
        row_copy(0, 0, 0, 0).wait()
        return c

    lax.fori_loop(0, tm // SUBLANES, issue, 0, unroll=ROW_UNROLL)
    lax.fori_loop(0, tm, drain, 0, unroll=SUBLANES)


def _scatter_kernel(hi_ref, lo_ref, src_ref, hs_in_ref, hs_ref, sem):
    del hs_in_ref

    def row_copy(i, j, hi, lo):
        return pltpu.make_async_copy(src_ref.at[i, pl.ds(j, 1)], hs_ref.at[hi, pl.ds(lo, 1)], sem)

    _row_copies(src_ref.shape[0] * SUBLANES, row_copy, hi_ref, lo_ref)


def _scatter(hi3, lo3, src, hs, tm):
    t8, _, w = src.shape
    ispec = pl.BlockSpec((1, 1, tm), lambda i: (i, 0, 0), memory_space=pltpu.SMEM)
    return pl.pallas_call(
        _scatter_kernel,
        grid=(t8 * SUBLANES // tm,),
        in_specs=[ispec, ispec, pl.BlockSpec((tm // SUBLANES, SUBLANES, w), lambda i: (i, 0, 0)),
                  pl.BlockSpec(memory_space=pl.ANY)],
        out_specs=pl.BlockSpec(memory_space=pl.ANY),
        out_shape=jax.ShapeDtypeStruct(hs.shape, hs.dtype),
        scratch_shapes=[pltpu.SemaphoreType.DMA(())],
        input_output_aliases={3: 0},
        compiler_params=_params("arbitrary"),
        name="scatter_rows",
    )(hi3, lo3, src, hs)


def _gather_res_kernel(hi_ref, lo_ref, x_ref, g_ref, ys_ref, o_ref, buf_ref, sem):
    def row_copy(i, j, hi, lo):
        return pltpu.make_async_copy(ys_ref.at[hi, pl.ds(lo, 1)], buf_ref.at[i, pl.ds(j, 1)], sem)

    _row_copies(buf_ref.shape[0] * SUBLANES, row_copy, hi_ref, lo_ref)
    o_ref[0] = x_ref[0] + g_ref[0] * buf_ref[...]


def _gather_res(hi3, lo3, x, g2, ys, tm):
    b, n, d = x.shape
    nt = n // tm
    x4 = x.reshape(b, n // SUBLANES, SUBLANES, d)
    ispec = pl.BlockSpec((1, 1, tm), lambda i, j: (i * nt + j, 0, 0), memory_space=pltpu.SMEM)
    tok = pl.BlockSpec((1, tm // SUBLANES, SUBLANES, d), lambda i, j: (i, j, 0, 0))
    out = pl.pallas_call(
        _gather_res_kernel,
        grid=(b, nt),
        in_specs=[ispec, ispec, tok, pl.BlockSpec((1, 1, 1, d), lambda i, j: (i, 0, 0, 0)),
                  pl.BlockSpec(memory_space=pl.ANY)],
        out_specs=tok,
        out_shape=jax.ShapeDtypeStruct(x4.shape, F32),
        scratch_shapes=[pltpu.VMEM((tm // SUBLANES, SUBLANES, d), F32), pltpu.SemaphoreType.DMA(())],
        compiler_params=_params("arbitrary", "arbitrary"),
        name="gather_residual",
    )(hi3, lo3, x4, g2[:, None], ys)
    return out.reshape(b, n, d)


def _experts_kernel(e1_ref, e2_ref, blk_ref, valid_ref, hs_ref, wg1_ref, wu1_ref, wd1_ref, wg2_ref, wu2_ref,
                    wd2_ref, o_ref, *, d_model):
    valid = valid_ref[pl.program_id(0)] > 0

    @pl.when(jnp.logical_not(valid))
    def _():
        o_ref[...] = jnp.zeros(o_ref.shape, F32)

    @pl.when(valid)
    def _():
        xb = hs_ref[:, 0:d_model].astype(BF16)

        def half(wg_ref, wu_ref, w):
            gte = _dot(xb, wg_ref[0])
            return (gte * _sigmoid(gte) * _dot(xb, wu_ref[0]) * w).astype(BF16)

        h1 = half(wg1_ref, wu1_ref, hs_ref[:, d_model:d_model + 1])
        h2 = half(wg2_ref, wu2_ref, hs_ref[:, d_model + 1:d_model + 2])
        o_ref[...] = _dot(h1, wd1_ref[0]) + _dot(h2, wd2_ref[0])


def _experts(e1, e2, blk, valid, hs, wg, wu, wd, tm):
    tp, w = hs.shape
    d = w - LANES
    f = wg.shape[-1]
    up1 = pl.BlockSpec((1, d, f), lambda i, e1, e2, blk, valid: (e1[i], 0, 0))
    up2 = pl.BlockSpec((1, d, f), lambda i, e1, e2, blk, valid: (e2[i], 0, 0))
    dn1 = pl.BlockSpec((1, f, d), lambda i, e1, e2, blk, valid: (e1[i], 0, 0))
    dn2 = pl.BlockSpec((1, f, d), lambda i, e1, e2, blk, valid: (e2[i], 0, 0))
    grid_spec = pltpu.PrefetchScalarGridSpec(
        num_scalar_prefetch=4, grid=(tp // tm,),
        in_specs=[pl.BlockSpec((tm, w), lambda i, e1, e2, blk, valid: (blk[i], 0)), up1, up1, dn1, up2, up2, dn2],
        out_specs=pl.BlockSpec((tm, d), lambda i, e1, e2, blk, valid: (i, 0)))
    return pl.pallas_call(
        functools.partial(_experts_kernel, d_model=d),
        grid_spec=grid_spec,
        out_shape=jax.ShapeDtypeStruct((tp, d), F32),
        compiler_params=_params("arbitrary"),
        name="moe_experts",
    )(e1, e2, blk, valid, hs, wg, wu, wd, wg, wu, wd)


PAIR_LO = np.array([0, 0, 0, 1, 1, 2], np.int32)
PAIR_HI = np.array([1, 2, 3, 2, 3, 3], np.int32)


def _moe_plan(counts, n_tiles, tm):
    c = counts[:N_BUCKETS]
    nt = (c + tm - 1) // tm
    tile_end = jnp.cumsum(nt)
    row_off = (tile_end - nt) * tm
    n_valid = tile_end[-1]
    i = jnp.arange(n_tiles, dtype=jnp.int32)
    blk = jnp.minimum(i, n_valid - 1)
    bkt = jnp.minimum(jnp.sum(blk[:, None] >= tile_end[None, :], axis=1), N_BUCKETS - 1).astype(jnp.int32)
    grp = bkt // N_PAIRS
    e1 = grp * EXPERTS_PER_GROUP + jnp.asarray(PAIR_LO)[bkt % N_PAIRS]
    e2 = grp * EXPERTS_PER_GROUP + jnp.asarray(PAIR_HI)[bkt % N_PAIRS]
    return row_off, e1, e2, blk.astype(jnp.int32), (i < n_valid).astype(jnp.int32)


def _moe(parts, cnt, wg, wu, wd, tm_exp):
    total = sum(p[0].shape[0] * p[0].shape[1] for p in parts)
    n_tiles = -(-total // tm_exp) + N_BUCKETS
    row_off, e1, e2, blk, valid = _moe_plan(cnt[:, 0].astype(jnp.int32), n_tiles, tm_exp)
    w = parts[0][2].shape[-1]
    hs = jnp.zeros((n_tiles * tm_exp // SUBLANES, SUBLANES, w), F32)
    addr = []
    for x, g2, hx, bkt, rank, tm in parts:
        start = sum(jnp.where(bkt == k, row_off[k], 0) for k in range(N_BUCKETS))
        pos = (start + rank).reshape(-1, 1, tm)
        hi, lo = pos // SUBLANES, pos % SUBLANES
        addr.append((hi, lo))
        hs = _scatter(hi, lo, hx.reshape(-1, SUBLANES, w), hs, tm)
    ys = _experts(e1, e2, blk, valid, hs.reshape(-1, w), wg, wu, wd, tm_exp)
    ys = ys.reshape(-1, SUBLANES, ys.shape[-1])
    return [_gather_res(hi, lo, x, g2, ys, tm) for (x, g2, hx, bkt, rank, tm), (hi, lo) in zip(parts, addr)]


def _relayout_w_in(w_in, d_model):
    sizes = (MLA_Q_RANK, MLA_KV_RANK, MLA_ROPE, 2 * CONV_CH, GQA_HEADS * GQA_HD, GQA_KV_HEADS * GQA_HD,
             GQA_KV_HEADS * GQA_HD, N_BRANCH * d_model)
    offs = np.cumsum((0,) + sizes)
    cq, ckv, kpe, conv, gq, gk, gv, gates = [w_in[..., offs[i]:offs[i + 1]] for i in range(len(sizes))]
    lead = w_in.shape[:-1]
    z = lambda n: jnp.zeros(lead + (n,), w_in.dtype)
    head = lambda hh: gq[..., hh * GQA_HD:(hh + 1) * GQA_HD]
    gq_slabs = [head(hh) for p in range(GQA_HEADS // 2) for hh in (p, p + GQA_HEADS // 2)]
    out = jnp.concatenate([cq, ckv, z(HALF), kpe, z(HALF - MLA_ROPE), gk, gv] + gq_slabs + [conv, gates], axis=-1)
    return out.astype(BF16)


def _relayout_uq(w_uq):
    l, r, _ = w_uq.shape
    w = w_uq.reshape(l, r, MLA_HEADS, MLA_QK)
    w = jnp.pad(w, ((0, 0), (0, 0), (0, 0), (0, LANES - MLA_QK)))
    return w.reshape(l, r, MLA_HEADS * LANES).astype(BF16)


def _relayout_ukv(w_ukv):
    l, r, _ = w_ukv.shape
    w = w_ukv.reshape(l, r, MLA_HEADS, MLA_NOPE + MLA_V)
    k = jnp.pad(w[..., :MLA_NOPE], ((0, 0), (0, 0), (0, 0), (0, LANES - MLA_NOPE)))
    v = w[..., MLA_NOPE:]
    return jnp.concatenate([k.reshape(l, r, MLA_HEADS * LANES), v.reshape(l, r, MLA_HEADS * MLA_V)],
                           axis=-1).astype(BF16)


def _rope_tables(n, r):
    rows = n // GRID_W
    row = jnp.repeat(jnp.arange(rows, dtype=jnp.int32), GRID_W).astype(F32)
    col = jnp.tile(jnp.arange(GRID_W, dtype=jnp.int32), rows).astype(F32)
    dim = r // 2
    inv = ROPE_BASE ** (-jnp.arange(0, dim, 2, dtype=F32) / dim)
    ar, ac = row[:, None] * inv[None, :], col[:, None] * inv[None, :]
    zero = jnp.zeros_like(ar)
    cos = jnp.concatenate([jnp.cos(ar), jnp.cos(ar), jnp.cos(ac), jnp.cos(ac)], axis=-1)
    sa = jnp.concatenate([-jnp.sin(ar), zero, -jnp.sin(ac), zero], axis=-1)
    sb = jnp.concatenate([zero, jnp.sin(ar), zero, jnp.sin(ac)], axis=-1)
    return cos, sa, sb


def _all_tables(n):
    mc, msa, msb = _rope_tables(n, MLA_ROPE)
    one = jnp.ones((n, MLA_NOPE), F32)
    zero = jnp.zeros((n, MLA_NOPE), F32)
    pad1 = jnp.ones((n, LANES - MLA_QK), F32)
    pad0 = jnp.zeros((n, LANES - MLA_QK), F32)
    mla = (jnp.concatenate([one, mc, pad1], -1), jnp.concatenate([zero, msa, pad0], -1),
           jnp.concatenate([zero, msb, pad0], -1))
    gqa = tuple(jnp.concatenate([t, t], -1) for t in _rope_tables(n, GQA_HD))
    return mla + gqa


def kernel(x, c, ctx, c_ctx, w_mod, b_mod, norm1_g, norm2_g, w_in, mla_q_norm, mla_kv_norm, mla_w_uq, mla_w_ukv,
           mla_q_gain, mla_k_gain, mla_w_o, conv_w, conv_b, conv_ln_g, conv_ln_b, conv_w_o, gqa_q_gain, gqa_k_gain,
           gqa_w_o, w_out, router_w, router_bias, moe_w_gate, moe_w_up, moe_w_down):
    b, n, d = x.shape
    n_ctx = ctx.shape[1]
    depth = w_mod.shape[0]
    assert n % GRID_W == 0 and n % 512 == 0 and n_ctx % CONV_CHUNK == 0 and d % LANES == 0

    tm = 512
    tm_ctx = min(n_ctx, 256)
    tq = 512
    tm_exp = 512

    m_rows = -(-(b + 1) // 8) * 8
    c_all = jnp.concatenate([c, c_ctx[None, :], jnp.zeros((m_rows - b - 1, d), F32)], axis=0)
    mod = _modulation(c_all, w_mod, b_mod)

    w_in_r = _relayout_w_in(w_in, d)
    w_uq_r = _relayout_uq(mla_w_uq)
    w_ukv_r = _relayout_ukv(mla_w_ukv)
    zl = lambda k: jnp.zeros((depth, k), F32)
    qg128 = jnp.concatenate([mla_q_gain, zl(LANES - MLA_QK)], -1)
    kg128 = jnp.concatenate([mla_k_gain, zl(LANES - MLA_QK)], -1)
    row1 = jnp.concatenate([mla_q_norm, mla_kv_norm, qg128, kg128, gqa_q_gain, gqa_q_gain, gqa_k_gain, gqa_k_gain,
                            zl(d - V_GGK - LANES)], -1)
    vec = jnp.stack([norm1_g, row1], axis=1)
    gqa_perm = np.concatenate([np.arange(GQA_HD) + GQA_HD * hh for pair in GQA_PAIRS for hh in pair])
    wm_o = mla_w_o.astype(BF16)
    wg_o = gqa_w_o[:, gqa_perm, :].astype(BF16)
    wc_o = conv_w_o.astype(BF16)
    wo = w_out.astype(BF16)
    cw = jnp.pad(conv_w, ((0, 0), (0, 32 - CONV_W), (0, 0)))
    cvec = jnp.pad(jnp.stack([conv_b, conv_ln_g, conv_ln_b], axis=1), ((0, 0), (0, 5), (0, 0)))
    rwt = router_w.T
    rw_hi = rwt.astype(BF16)
    rw = jnp.stack([rw_hi, (rwt - rw_hi.astype(F32)).astype(BF16)])
    rb = router_bias.reshape(N_EXPERTS, 1)
    wg_e, wu_e, wd_e = moe_w_gate.astype(BF16), moe_w_up.astype(BF16), moe_w_down.astype(BF16)
    tables = _all_tables(n)

    xc = ctx
    for l in range(depth):
        last = l == depth - 1
        sh1, sc1, g1, sh2, sc2, g2 = [t[:, None, :] for t in jnp.split(mod[l, :b], 6, axis=-1)]
        csh1, csc1, cg1, csh2, csc2, cg2 = [jnp.broadcast_to(t[None, :, :], (b, 1, d))
                                            for t in jnp.split(mod[l, b:b + 1], 6, axis=-1)]
        qm, km, vm, qg, kg, vg, y, gates = _pre_mixer(x, sc1, sh1, vec[l], w_in_r[l], w_uq_r[l], w_ukv_r[l],
                                                      tables, tm)
        cqm, ckm, cvm, cqg, ckg, cvg, cy, cgates = _pre_mixer(xc, csc1, csh1, vec[l], w_in_r[l], w_uq_r[l],
                                                              w_ukv_r[l], None, tm_ctx)
        kg4, vg4, ckg4, cvg4 = kg[:, None], vg[:, None], ckg[:, None], cvg[:, None]
        o_mla = _attention(qm, [km, vm, ckm, cvm], MLA_HEAD_MAP, MLA_PAIRS, tq, "attn_mla")
        o_gqa = _attention(qg, [kg4, vg4, ckg4, cvg4], GQA_HEAD_MAP, GQA_PAIRS, tq, "attn_gqa")
        z = _conv(y, cw[l], cvec[l])
        n2 = norm2_g[l][None, :]
        out_w = (wm_o[l], wc_o[l], wg_o[l], wo[l])
        x, hx, bkt, rank, cnt = _merge_route(x, g1, o_mla, z, o_gqa, gates, *out_w, sc2, sh2, n2, rw, rb,
                                             jnp.zeros((BUCKET_ROWS, LANES), F32), tm)
        moe_parts = [(x, g2, hx, bkt, rank, tm)]
        if not last:
            co_mla = _attention(cqm, [ckm, cvm], MLA_HEAD_MAP, MLA_PAIRS, tm_ctx, "attn_mla_ctx")
            co_gqa = _attention(cqg, [ckg4, cvg4], GQA_HEAD_MAP, GQA_PAIRS, tm_ctx, "attn_gqa_ctx")
            cz = _conv(cy, cw[l], cvec[l])
            xc, chx, cbkt, crank, cnt = _merge_route(xc, cg1, co_mla, cz, co_gqa, cgates, *out_w, csc2, csh2, n2,
                                                     rw, rb, cnt, tm_ctx)
            moe_parts.append((xc, cg2, chx, cbkt, crank, tm_ctx))
        outs = _moe(moe_parts, cnt, wg_e[l], wu_e[l], wd_e[l], tm_exp)
        x = outs[0]
        if not last:
            xc = outs[1]
    return x
```

```python
import functools
import math

import numpy as np
import jax
import jax.numpy as jnp
from jax import lax
from jax.experimental import pallas as pl
from jax.experimental.pallas import tpu as pltpu

GRID_W = 64
ROPE_BASE = 10000.0
EPS = 1e-6

MLA_HEADS = 8
MLA_Q_RANK = 256
MLA_KV_RANK = 128
MLA_NOPE = 64
MLA_ROPE = 32
MLA_V = 64
MLA_QK = MLA_NOPE + MLA_ROPE
CONV_CH = 512
CONV_W = 31
GQA_HEADS = 8
GQA_KV_HEADS = 2
GQA_HD = 64
N_BRANCH = 3
N_EXPERTS = 16
N_GROUPS = 4
EXPERTS_PER_GROUP = 4
D_EXPERT = 512

LANES = 128
SUBLANES = 8
HALF = LANES // 2
N_PAIRS = 6
N_BUCKETS = N_GROUPS * N_PAIRS
BUCKET_ROWS = 32
LOG2E = math.log2(math.e)
VMEM_LIMIT = 56 * 1024 * 1024

C_CQ = 0
C_CKV = C_CQ + MLA_Q_RANK
C_KPE = C_CKV + MLA_KV_RANK
C_GK = C_KPE + LANES
C_GV = C_GK + LANES
C_GQ = C_GV + LANES
C_CA = C_GQ + GQA_HEADS * GQA_HD
C_CG = C_CA + CONV_CH
C_GATE = C_CG + CONV_CH
D_IN_R = C_GATE

V_QN = 0
V_KVN = 256
V_QG = 384
V_KG = 512
V_GGQ = 640
V_GGK = 768

BF16 = jnp.bfloat16
F32 = jnp.float32
NT_DIMS = (((1,), (1,)), ((), ()))


def _dot(a, b):
    return jnp.dot(a, b, preferred_element_type=F32)


def _sigmoid(v):
    return 0.5 * jnp.tanh(0.5 * v) + 0.5


def _split_hi_lo(a):
    hi = a.astype(BF16)
    lo = (a - hi.astype(F32)).astype(BF16)
    return hi, lo


def _params(*sem):
    return pltpu.CompilerParams(dimension_semantics=sem, vmem_limit_bytes=VMEM_LIMIT)


def _const_spec(shape):
    nd = len(shape)
    return pl.BlockSpec(shape, lambda *_: (0,) * nd)


def _mod_kernel(c_ref, w_ref, b_ref, o_ref):
    c = c_ref[...]
    s = c * _sigmoid(c)
    sh, sl = _split_hi_lo(s)
    wh, wl = _split_hi_lo(w_ref[0])
    o_ref[0] = _dot(sh, wh) + _dot(sh, wl) + _dot(sl, wh) + b_ref[0]


def _modulation(c_all, w_mod, b_mod):
    depth, d, d6 = w_mod.shape
    m = c_all.shape[0]
    tn = 1536
    return pl.pallas_call(
        _mod_kernel,
        grid=(depth, d6 // tn),
        in_specs=[pl.BlockSpec((m, d), lambda l, j: (0, 0)),
                  pl.BlockSpec((1, d, tn), lambda l, j: (l, 0, j)),
                  pl.BlockSpec((1, 1, tn), lambda l, j: (l, 0, j))],
        out_specs=pl.BlockSpec((1, m, tn), lambda l, j: (l, 0, j)),
        out_shape=jax.ShapeDtypeStruct((depth, m, d6), F32),
        compiler_params=_params("parallel", "parallel"),
        name="modulation",
    )(c_all, w_mod, b_mod.reshape(depth, 1, d6))


def _rope(n, cos, sa, sb, quarter):
    return n * cos + pltpu.roll(n, LANES - quarter, 1) * sa + pltpu.roll(n, quarter, 1) * sb


def _pre_mixer_kernel(x_ref, sc_ref, sh_ref, vec_ref, w_ref, wuq_ref, wukv_ref, *rest, use_rope, d_model, t_sub):
    for r0 in range(0, x_ref.shape[1], t_sub):
        _pre_mixer_rows(slice(r0, r0 + t_sub), x_ref, sc_ref, sh_ref, vec_ref, w_ref, wuq_ref, wukv_ref, rest,
                        use_rope, d_model)


def _pre_mixer_rows(rows, x_ref, sc_ref, sh_ref, vec_ref, w_ref, wuq_ref, wukv_ref, rest, use_rope, d_model):
    if use_rope:
        mc_ref, msa_ref, msb_ref, gc_ref, gsa_ref, gsb_ref = rest[:6]
        rest = rest[6:]
    qm_ref, km_ref, vm_ref, qg_ref, kg_ref, vg_ref, y_ref, gate_ref = rest

    x = x_ref[0, rows, :]
    inv = lax.rsqrt(jnp.mean(x * x, axis=-1, keepdims=True) + EPS)
    h = (x * inv * vec_ref[0:1, :]) * (1.0 + sc_ref[0]) + sh_ref[0]
    hb = h.astype(BF16)

    lane = lax.broadcasted_iota(jnp.int32, (1, LANES), 1)
    lo = lane < HALF

    def proj(c0, width):
        return _dot(hb, w_ref[:, c0:c0 + width])

    def rms(v, g):
        return v * lax.rsqrt(jnp.mean(v * v, axis=-1, keepdims=True) + EPS) * g

    def seg_inv(xs, n_lo, n_hi):
        sq = xs * xs
        if n_hi is None:
            return lax.rsqrt(jnp.sum(sq, axis=-1, keepdims=True) * (1.0 / n_lo) + EPS)
        s_lo = jnp.sum(jnp.where(lo, sq, 0.0), axis=-1, keepdims=True)
        s_hi = jnp.sum(jnp.where(lo, 0.0, sq), axis=-1, keepdims=True)
        return jnp.where(lo, lax.rsqrt(s_lo * (1.0 / n_lo) + EPS), lax.rsqrt(s_hi * (1.0 / n_hi) + EPS))

    def mla_rope(n):
        return _rope(n, mc_ref[rows, :], msa_ref[rows, :], msb_ref[rows, :], MLA_ROPE // 4) if use_rope else n

    def gqa_rope(n):
        return _rope(n, gc_ref[rows, :], gsa_ref[rows, :], gsb_ref[rows, :], GQA_HD // 4) if use_rope else n

    small = proj(0, C_CA)

    def narrow(c0, width):
        return small[:, c0:c0 + width]

    cqn = rms(narrow(C_CQ, MLA_Q_RANK), vec_ref[1:2, V_QN:V_QN + MLA_Q_RANK])
    q = _dot(cqn.astype(BF16), wuq_ref[...])
    qgain = vec_ref[1:2, V_QG:V_QG + LANES]
    for hh in range(MLA_HEADS):
        xs = q[:, hh * LANES:(hh + 1) * LANES]
        n = mla_rope(xs * seg_inv(xs, MLA_NOPE, MLA_ROPE) * qgain)
        qm_ref[0, hh, rows, :] = (n * (MLA_QK ** -0.5 * LOG2E)).astype(BF16)

    kgain = vec_ref[1:2, V_KG:V_KG + LANES]
    kpe = narrow(C_KPE, LANES)
    kpe = mla_rope(kpe * seg_inv(kpe, MLA_ROPE, None) * kgain)
    ckvn = rms(narrow(C_CKV, MLA_KV_RANK), vec_ref[1:2, V_KVN:V_KVN + MLA_KV_RANK])
    kv = _dot(ckvn.astype(BF16), wukv_ref[...])
    for hh in range(MLA_HEADS):
        xs = kv[:, hh * LANES:(hh + 1) * LANES]
        km_ref[0, hh, rows, :] = (xs * seg_inv(xs, MLA_NOPE, None) * kgain + kpe).astype(BF16)
    ones_slab = jnp.ones((x.shape[0], LANES), BF16)
    v0 = MLA_HEADS * LANES
    for p in range(MLA_HEADS // 2):
        vm_ref[0, p, rows, 0:LANES] = kv[:, v0 + p * LANES:v0 + (p + 1) * LANES].astype(BF16)
        vm_ref[0, p, rows, LANES:2 * LANES] = ones_slab

    gk = narrow(C_GK, LANES)
    ggk = vec_ref[1:2, V_GGK:V_GGK + LANES]
    kg_ref[0, rows, :] = gqa_rope(gk * seg_inv(gk, GQA_HD, GQA_HD) * ggk).astype(BF16)
    vg_ref[0, rows, 0:LANES] = narrow(C_GV, LANES).astype(BF16)
    vg_ref[0, rows, LANES:2 * LANES] = ones_slab
    ggq = vec_ref[1:2, V_GGQ:V_GGQ + LANES]
    for p in range(GQA_HEADS // 2):
        xs = narrow(C_GQ + p * LANES, LANES)
        n = gqa_rope(xs * seg_inv(xs, GQA_HD, GQA_HD) * ggq)
        qg_ref[0, p, rows, :] = (n * (GQA_HD ** -0.5 * LOG2E)).astype(BF16)

    y_ref[0, rows, :] = proj(C_CA, CONV_CH) * _sigmoid(proj(C_CG, CONV_CH))
    for j in range(N_BRANCH):
        gate_ref[0, rows, j * d_model:(j + 1) * d_model] = _sigmoid(
            proj(C_GATE + j * d_model, d_model)).astype(BF16)


PRE_SUB = 512


def _pre_mixer(x, sc, sh, vec, w_in_r, w_uq_r, w_ukv_r, tables, tm):
    b, n, d = x.shape
    use_rope = tables is not None
    nt = n // tm
    bmap = lambda i, j: (j, 0, 0)
    in_specs = [pl.BlockSpec((1, tm, d), lambda i, j: (j, i, 0)),
                pl.BlockSpec((1, 1, d), bmap),
                pl.BlockSpec((1, 1, d), bmap),
                _const_spec(vec.shape), _const_spec(w_in_r.shape), _const_spec(w_uq_r.shape),
                _const_spec(w_ukv_r.shape)]
    args = [x, sc, sh, vec, w_in_r, w_uq_r, w_ukv_r]
    if use_rope:
        in_specs += [pl.BlockSpec((tm, LANES), lambda i, j: (i, 0))] * 6
        args += list(tables)
    hspec = lambda nh, w=LANES: pl.BlockSpec((1, nh, tm, w), lambda i, j: (j, 0, i, 0))
    tspec = lambda w: pl.BlockSpec((1, tm, w), lambda i, j: (j, i, 0))
    out_specs = [hspec(MLA_HEADS), hspec(MLA_HEADS), hspec(MLA_HEADS // 2, 2 * LANES), hspec(GQA_HEADS // 2),
                 tspec(LANES), tspec(2 * LANES), tspec(CONV_CH), tspec(N_BRANCH * d)]
    sds = jax.ShapeDtypeStruct
    out_shape = [sds((b, MLA_HEADS, n, LANES), BF16), sds((b, MLA_HEADS, n, LANES), BF16),
                 sds((b, MLA_HEADS // 2, n, 2 * LANES), BF16), sds((b, GQA_HEADS // 2, n, LANES), BF16),
                 sds((b, n, LANES), BF16), sds((b, n, 2 * LANES), BF16),
                 sds((b, n, CONV_CH), F32), sds((b, n, N_BRANCH * d), BF16)]
    return pl.pallas_call(
        functools.partial(_pre_mixer_kernel, use_rope=use_rope, d_model=d, t_sub=min(tm, PRE_SUB)),
        grid=(nt, b), in_specs=in_specs, out_specs=out_specs, out_shape=out_shape,
        compiler_params=_params("parallel", "parallel"),
        name="pre_mixer_rope" if use_rope else "pre_mixer",
    )(*args)


def _attn_kernel(q_ref, *refs, n_seg, head_map, pairs, t_sub, t_piece):
    kv_refs = refs[:2 * n_seg]
    o_ref = refs[2 * n_seg]
    lane = lax.broadcasted_iota(jnp.int32, (1, LANES), 1)
    lo = lane < HALF

    def one_head(hh, rows):
        qidx, q_half, kidx, vidx = head_map[hh]
        q = q_ref[0, qidx, rows, :]
        if q_half is not None:
            q = jnp.where(lo if q_half == 0 else jnp.logical_not(lo), q, jnp.zeros_like(q))
        ss_all = [lax.dot_general(q, kv_refs[2 * s][0, kidx], NT_DIMS, preferred_element_type=F32)
                  for s in range(n_seg)]
        outs = []
        for p0 in range(0, ss_all[0].shape[0], t_piece):
            ss = [s[p0:p0 + t_piece] for s in ss_all]
            m = functools.reduce(jnp.maximum, [jnp.max(s, axis=-1, keepdims=True) for s in ss])
            ol = functools.reduce(jnp.add, [_dot(jnp.exp2(ss[s] - m).astype(BF16), kv_refs[2 * s + 1][0, vidx])
                                            for s in range(n_seg)])
            outs.append(ol[:, 0:LANES] / ol[:, LANES:LANES + 1])
        return outs[0] if len(outs) == 1 else jnp.concatenate(outs, axis=0)

    for r0 in range(0, q_ref.shape[2], t_sub):
        rows = slice(r0, r0 + t_sub)
        for p, (h_lo, h_hi) in enumerate(pairs):
            o_ref[0, rows, p * LANES:(p + 1) * LANES] = jnp.where(
                lo, one_head(h_lo, rows), one_head(h_hi, rows)).astype(BF16)


MLA_HEAD_MAP = tuple((h, None, h, h // 2) for h in range(MLA_HEADS))
MLA_PAIRS = tuple((2 * p, 2 * p + 1) for p in range(MLA_HEADS // 2))
GQA_HEAD_MAP = tuple((h % (GQA_HEADS // 2), h // (GQA_HEADS // 2), 0, 0) for h in range(GQA_HEADS))
GQA_PAIRS = tuple((a, a + GQA_HEADS // 2) for a in range(GQA_HEADS // 2))


ATTN_SUB = 512
ATTN_PIECE = 256


def _attention(q, kvs, head_map, pairs, tq, name):
    b, nh, n, _ = q.shape
    in_specs = [pl.BlockSpec((1, nh, tq, LANES), lambda i, j: (i, 0, j, 0))]
    args = [q]
    for a in kvs:
        in_specs.append(pl.BlockSpec((1,) + a.shape[1:], lambda i, j: (i, 0, 0, 0)))
        args.append(a)
    width = len(pairs) * LANES
    return pl.pallas_call(
        functools.partial(_attn_kernel, n_seg=len(kvs) // 2, head_map=head_map, pairs=pairs,
                          t_sub=min(tq, ATTN_SUB), t_piece=min(tq, ATTN_PIECE)),
        grid=(b, n // tq), in_specs=in_specs,
        out_specs=pl.BlockSpec((1, tq, width), lambda i, j: (i, j, 0)),
        out_shape=jax.ShapeDtypeStruct((b, n, width), BF16),
        compiler_params=_params("parallel", "parallel"),
        name=name,
    )(*args)


CONV_PAD = 16
CONV_TAIL = 24
CONV_CHUNK = 256


def _conv_kernel(y_ref, cw_ref, vec_ref, z_ref, buf_ref, ph_ref, *, n):
    ch = y_ref.shape[-1]
    buf_ref[0:CONV_PAD, :] = jnp.zeros((CONV_PAD, ch), F32)
    buf_ref[CONV_PAD + n:CONV_PAD + n + CONV_TAIL, :] = jnp.zeros((CONV_TAIL, ch), F32)
    buf_ref[CONV_PAD:CONV_PAD + n, :] = y_ref[0]
    first = CONV_PAD - CONV_W // 2
    rows = ph_ref.shape[1] - CONV_TAIL
    win_rows = rows + CONV_W + 1

    def chunk(c, carry):
        r0 = pl.multiple_of(c * rows, rows)
        win = buf_ref[pl.ds(r0, win_rows), :]
        for s in range(SUBLANES):
            ph_ref[s] = win[s:s + ph_ref.shape[1], :]
        acc = jnp.zeros((rows, ch), F32) + vec_ref[0:1, :]
        for k in range(CONV_W):
            a, s = divmod(first + k, SUBLANES)
            acc = acc + ph_ref[s, a * SUBLANES:a * SUBLANES + rows, :] * cw_ref[k:k + 1, :]
        mu = jnp.mean(acc, axis=-1, keepdims=True)
        cen = acc - mu
        var = jnp.mean(cen * cen, axis=-1, keepdims=True)
        yn = cen * lax.rsqrt(var + EPS) * vec_ref[1:2, :] + vec_ref[2:3, :]
        z_ref[0, pl.ds(r0, rows), :] = (yn * _sigmoid(yn)).astype(BF16)
        return carry

    lax.fori_loop(0, n // rows, chunk, 0)


def _conv(y, cw, cvec):
    b, n, ch = y.shape
    chunk = min(CONV_CHUNK, n)
    assert n % chunk == 0 and chunk % SUBLANES == 0
    return pl.pallas_call(
        functools.partial(_conv_kernel, n=n),
        grid=(b,),
        in_specs=[pl.BlockSpec((1, n, ch), lambda i: (i, 0, 0)), _const_spec(cw.shape), _const_spec(cvec.shape)],
        out_specs=pl.BlockSpec((1, n, ch), lambda i: (i, 0, 0)),
        out_shape=jax.ShapeDtypeStruct((b, n, ch), BF16),
        scratch_shapes=[pltpu.VMEM((CONV_PAD + n + CONV_TAIL, ch), F32),
                        pltpu.VMEM((SUBLANES, chunk + CONV_TAIL, ch), F32)],
        compiler_params=_params("parallel"),
        name="conv",
    )(y, cw, cvec)


def _merge_route_kernel(x_ref, g_ref, om_ref, oc_ref, og_ref, gate_ref, wm_ref, wc_ref, wg_ref, wo_ref,
                        sc_ref, sh_ref, n2_ref, rw_ref, rb_ref, cin_ref,
                        o_ref, hx_ref, bkt_ref, rank_ref, cnt_ref, *, d_model):
    d = d_model
    merged = (gate_ref[0, :, 0:d].astype(F32) * _dot(om_ref[0], wm_ref[...])
              + gate_ref[0, :, d:2 * d].astype(F32) * _dot(oc_ref[0], wc_ref[...])
              + gate_ref[0, :, 2 * d:3 * d].astype(F32) * _dot(og_ref[0], wg_ref[...]))
    x_new = x_ref[0] + g_ref[0] * _dot(merged.astype(BF16), wo_ref[...])
    o_ref[0] = x_new
    _route(x_new, sc_ref, sh_ref, n2_ref, rw_ref, rb_ref, cin_ref, hx_ref, bkt_ref, rank_ref, cnt_ref, d)


def _merge_route(x, g1, o_mla, z, o_gqa, gates, wm, wc, wg, wo, sc2, sh2, n2, rw, rb, cin, tm):
    b, n, d = x.shape
    nt = n // tm
    sds = jax.ShapeDtypeStruct
    tok = lambda w: pl.BlockSpec((1, tm, w), lambda i, j: (i, j, 0))
    per_b = pl.BlockSpec((1, 1, d), lambda i, j: (i, 0, 0))
    ispec = pl.BlockSpec((1, 1, 1, tm), lambda i, j: (i, j, 0, 0))
    return pl.pallas_call(
        functools.partial(_merge_route_kernel, d_model=d),
        grid=(b, nt),
        in_specs=[tok(d), per_b, tok(o_mla.shape[-1]), tok(z.shape[-1]), tok(o_gqa.shape[-1]), tok(N_BRANCH * d),
                  _const_spec(wm.shape), _const_spec(wc.shape), _const_spec(wg.shape), _const_spec(wo.shape),
                  per_b, per_b, _const_spec(n2.shape), _const_spec(rw.shape), _const_spec(rb.shape),
                  _const_spec(cin.shape)],
        out_specs=[tok(d), tok(d + LANES), ispec, ispec, _const_spec((BUCKET_ROWS, LANES))],
        out_shape=[sds((b, n, d), F32), sds((b, n, d + LANES), F32), sds((b, nt, 1, tm), jnp.int32),
                   sds((b, nt, 1, tm), jnp.int32), sds((BUCKET_ROWS, LANES), F32)],
        compiler_params=_params("arbitrary", "arbitrary"),
        name="merge_route",
    )(x, g1, o_mla, z, o_gqa, gates, wm, wc, wg, wo, sc2, sh2, n2, rw, rb, cin)


def _first_argmax4(v):
    m = functools.reduce(jnp.maximum, v)
    idx = jnp.where(v[0] == m, 0, jnp.where(v[1] == m, 1, jnp.where(v[2] == m, 2, 3)))
    return m, idx


def _pick4(idx, v):
    return jnp.where(idx == 0, v[0], jnp.where(idx == 1, v[1], jnp.where(idx == 2, v[2], v[3])))


def _route(x, sc_ref, sh_ref, g_ref, rw_ref, rb_ref, cin_ref, hx_ref, bkt_ref, rank_ref, cnt_ref, d_model):
    tm = x.shape[0]
    first = (pl.program_id(0) == 0) & (pl.program_id(1) == 0)

    @pl.when(first)
    def _():
        cnt_ref[...] = cin_ref[...]

    inv = lax.rsqrt(jnp.mean(x * x, axis=-1, keepdims=True) + EPS)
    h = (x * inv * g_ref[...]) * (1.0 + sc_ref[0]) + sh_ref[0]
    hx_ref[0, :, 0:d_model] = h

    hh, hl = _split_hi_lo(h)
    rwh, rwl = rw_ref[0], rw_ref[1]
    logits = (lax.dot_general(rwh, hh, NT_DIMS, preferred_element_type=F32)
              + lax.dot_general(rwh, hl, NT_DIMS, preferred_element_type=F32)
              + lax.dot_general(rwl, hh, NT_DIMS, preferred_element_type=F32))
    s_all = _sigmoid(logits)
    sel_all = s_all + rb_ref[...]
    s = [s_all[e:e + 1, :] for e in range(N_EXPERTS)]
    sel = [sel_all[e:e + 1, :] for e in range(N_EXPERTS)]

    gscore = []
    for g in range(N_GROUPS):
        a, b, c, d = sel[4 * g:4 * g + 4]
        gscore.append(functools.reduce(jnp.maximum, [a + b, a + c, a + d, b + c, b + d, c + d]))
    _, gi = _first_argmax4(gscore)
    v = [_pick4(gi, [sel[4 * g + j] for g in range(N_GROUPS)]) for j in range(EXPERTS_PER_GROUP)]
    u = [_pick4(gi, [s[4 * g + j] for g in range(N_GROUPS)]) for j in range(EXPERTS_PER_GROUP)]
    _, i1 = _first_argmax4(v)
    _, i2 = _first_argmax4([jnp.where(i1 == j, -jnp.inf, v[j]) for j in range(EXPERTS_PER_GROUP)])
    e_lo = jnp.minimum(i1, i2)
    e_hi = jnp.maximum(i1, i2)
    pair = jnp.where(e_lo == 0, e_hi - 1, jnp.where(e_lo == 1, e_hi + 1, N_PAIRS - 1))
    bucket = gi * N_PAIRS + pair
    u_lo = _pick4(e_lo, u)
    u_hi = _pick4(e_hi, u)
    den = u_lo + u_hi
    wts = jnp.concatenate([u_lo / den, u_hi / den, jnp.zeros((LANES - 2, tm), F32)], axis=0)
    hx_ref[0, :, d_model:d_model + LANES] = wts.T

    onehot = lax.broadcasted_iota(jnp.int32, (BUCKET_ROWS, tm), 0) == bucket
    oh = jnp.where(onehot, 1.0, 0.0)
    before = lax.broadcasted_iota(jnp.int32, (tm, tm), 0) < lax.broadcasted_iota(jnp.int32, (tm, tm), 1)
    cum = _dot(oh.astype(BF16), jnp.where(before, 1.0, 0.0).astype(BF16))
    rank = jnp.sum(oh * (cum + cnt_ref[:, 0:1]), axis=0, keepdims=True)
    bkt_ref[0, 0] = bucket
    rank_ref[0, 0] = rank.astype(jnp.int32)
    cnt_ref[...] = cnt_ref[...] + jnp.sum(oh, axis=1, keepdims=True)


ROW_UNROLL = 2
N_DMA_PRIORITIES = 2


def _issue_rows(tm, row_copy, hi_ref, lo_ref):
    def issue(i, c):
        for j in range(SUBLANES):
            r = i * SUBLANES + j
            lo = lo_ref[0, 0, r] & (SUBLANES - 1)
            row_copy(i, j, hi_ref[0, 0, r], lo).start(priority=j % N_DMA_PRIORITIES)
        return c

    lax.fori_loop(0, tm // SUBLANES, issue, 0, unroll=ROW_UNROLL)


def _drain_rows(tm, row_copy):
    def drain(r, c):
        row_copy(0, 0, 0, 0).wait()
        return c

    lax.fori_loop(0, tm, drain, 0, unroll=SUBLANES)


def _scatter_kernel(hi_ref, lo_ref, src_ref, hs_in_ref, hs_ref, sems, *, tm):
    del hs_in_ref
    step = pl.program_id(0)
    last = pl.num_programs(0) - 1
    base = step * (tm // SUBLANES)

    def row_copy_on(slot):
        def row_copy(i, j, hi, lo):
            return pltpu.make_async_copy(src_ref.at[base + i, pl.ds(j, 1)], hs_ref.at[hi, pl.ds(lo, 1)],
                                         sems.at[slot])
        return row_copy

    slot = step % 2
    _issue_rows(tm, row_copy_on(slot), hi_ref, lo_ref)

    @pl.when(step > 0)
    def _():
        _drain_rows(tm, row_copy_on(1 - slot))

    @pl.when(step == last)
    def _():
        _drain_rows(tm, row_copy_on(slot))


def _scatter(hi3, lo3, src, hs, tm):
    t8 = src.shape[0]
    ispec = pl.BlockSpec((1, 1, tm), lambda i: (i, 0, 0), memory_space=pltpu.SMEM)
    return pl.pallas_call(
        functools.partial(_scatter_kernel, tm=tm),
        grid=(t8 * SUBLANES // tm,),
        in_specs=[ispec, ispec, pl.BlockSpec(memory_space=pl.ANY), pl.BlockSpec(memory_space=pl.ANY)],
        out_specs=pl.BlockSpec(memory_space=pl.ANY),
        out_shape=jax.ShapeDtypeStruct(hs.shape, hs.dtype),
        scratch_shapes=[pltpu.SemaphoreType.DMA((2,))],
        input_output_aliases={3: 0},
        compiler_params=_params("arbitrary"),
        name="scatter_rows",
    )(hi3, lo3, src, hs)


def _gather_res_kernel(hic_ref, loc_ref, hin_ref, lon_ref, x_ref, g_ref, ys_ref, o_ref, buf_ref, sems):
    step = pl.program_id(0)
    tm = buf_ref.shape[1] * SUBLANES

    def row_copy_on(slot):
        def row_copy(i, j, hi, lo):
            return pltpu.make_async_copy(ys_ref.at[hi, pl.ds(lo, 1)], buf_ref.at[slot, i, pl.ds(j, 1)],
                                         sems.at[slot])
        return row_copy

    slot = step % 2

    @pl.when(step == 0)
    def _():
        _issue_rows(tm, row_copy_on(slot), hic_ref, loc_ref)

    @pl.when(step + 1 < pl.num_programs(0))
    def _():
        _issue_rows(tm, row_copy_on(1 - slot), hin_ref, lon_ref)

    _drain_rows(tm, row_copy_on(slot))
    o_ref[0] = x_ref[0] + g_ref[0] * buf_ref[slot]


def _gather_res(hi3, lo3, x, g2, ys, tm):
    b, n, d = x.shape
    nt = n // tm
    steps = b * nt
    x4 = x.reshape(b, n // SUBLANES, SUBLANES, d)
    cur = pl.BlockSpec((1, 1, tm), lambda i: (i, 0, 0), memory_space=pltpu.SMEM)
    nxt = pl.BlockSpec((1, 1, tm), lambda i: (jnp.minimum(i + 1, steps - 1), 0, 0), memory_space=pltpu.SMEM)
    tok = pl.BlockSpec((1, tm // SUBLANES, SUBLANES, d), lambda i: (i // nt, i % nt, 0, 0))
    out = pl.pallas_call(
        _gather_res_kernel,
        grid=(steps,),
        in_specs=[cur, cur, nxt, nxt, tok, pl.BlockSpec((1, 1, 1, d), lambda i: (i // nt, 0, 0, 0)),
                  pl.BlockSpec(memory_space=pl.ANY)],
        out_specs=tok,
        out_shape=jax.ShapeDtypeStruct(x4.shape, F32),
        scratch_shapes=[pltpu.VMEM((2, tm // SUBLANES, SUBLANES, d), F32), pltpu.SemaphoreType.DMA((2,))],
        compiler_params=_params("arbitrary"),
        name="gather_residual",
    )(hi3, lo3, hi3, lo3, x4, g2[:, None], ys)
    return out.reshape(b, n, d)


def _experts_kernel(e1_ref, e2_ref, blk_ref, valid_ref, hs_ref, wg1_ref, wu1_ref, wd1_ref, wg2_ref, wu2_ref,
                    wd2_ref, o_ref, *, d_model):
    valid = valid_ref[pl.program_id(0)] > 0

    @pl.when(jnp.logical_not(valid))
    def _():
        o_ref[...] = jnp.zeros(o_ref.shape, F32)

    @pl.when(valid)
    def _():
        xb = hs_ref[:, 0:d_model].astype(BF16)

        def half(wg_ref, wu_ref, w):
            gte = _dot(xb, wg_ref[0])
            return (gte * _sigmoid(gte) * _dot(xb, wu_ref[0]) * w).astype(BF16)

        h1 = half(wg1_ref, wu1_ref, hs_ref[:, d_model:d_model + 1])
        h2 = half(wg2_ref, wu2_ref, hs_ref[:, d_model + 1:d_model + 2])
        o_ref[...] = _dot(h1, wd1_ref[0]) + _dot(h2, wd2_ref[0])


def _experts(e1, e2, blk, valid, hs, wg, wu, wd, tm):
    tp, w = hs.shape
    d = w - LANES
    f = wg.shape[-1]
    up1 = pl.BlockSpec((1, d, f), lambda i, e1, e2, blk, valid: (e1[i], 0, 0))
    up2 = pl.BlockSpec((1, d, f), lambda i, e1, e2, blk, valid: (e2[i], 0, 0))
    dn1 = pl.BlockSpec((1, f, d), lambda i, e1, e2, blk, valid: (e1[i], 0, 0))
    dn2 = pl.BlockSpec((1, f, d), lambda i, e1, e2, blk, valid: (e2[i], 0, 0))
    grid_spec = pltpu.PrefetchScalarGridSpec(
        num_scalar_prefetch=4, grid=(tp // tm,),
        in_specs=[pl.BlockSpec((tm, w), lambda i, e1, e2, blk, valid: (blk[i], 0)), up1, up1, dn1, up2, up2, dn2],
        out_specs=pl.BlockSpec((tm, d), lambda i, e1, e2, blk, valid: (i, 0)))
    return pl.pallas_call(
        functools.partial(_experts_kernel, d_model=d),
        grid_spec=grid_spec,
        out_shape=jax.ShapeDtypeStruct((tp, d), F32),
        compiler_params=_params("arbitrary"),
        name="moe_experts",
    )(e1, e2, blk, valid, hs, wg, wu, wd, wg, wu, wd)


PAIR_LO = np.array([0, 0, 0, 1, 1, 2], np.int32)
PAIR_HI = np.array([1, 2, 3, 2, 3, 3], np.int32)


def _moe_plan(counts, n_tiles, tm):
    c = counts[:N_BUCKETS]
    nt = (c + tm - 1) // tm
    tile_end = jnp.cumsum(nt)
    row_off = (tile_end - nt) * tm
    n_valid = tile_end[-1]
    i = jnp.arange(n_tiles, dtype=jnp.int32)
    blk = jnp.minimum(i, n_valid - 1)
    bkt = jnp.minimum(jnp.sum(blk[:, None] >= tile_end[None, :], axis=1), N_BUCKETS - 1).astype(jnp.int32)
    grp = bkt // N_PAIRS
    e1 = grp * EXPERTS_PER_GROUP + jnp.asarray(PAIR_LO)[bkt % N_PAIRS]
    e2 = grp * EXPERTS_PER_GROUP + jnp.asarray(PAIR_HI)[bkt % N_PAIRS]
    return row_off, e1, e2, blk.astype(jnp.int32), (i < n_valid).astype(jnp.int32)


def _moe(parts, cnt, wg, wu, wd, tm_exp):
    total = sum(p[0].shape[0] * p[0].shape[1] for p in parts)
    n_tiles = -(-total // tm_exp) + N_BUCKETS
    row_off, e1, e2, blk, valid = _moe_plan(cnt[:, 0].astype(jnp.int32), n_tiles, tm_exp)
    w = parts[0][2].shape[-1]
    hs = jnp.zeros((n_tiles * tm_exp // SUBLANES, SUBLANES, w), F32)
    addr = []
    for x, g2, hx, bkt, rank, tm in parts:
        start = sum(jnp.where(bkt == k, row_off[k], 0) for k in range(N_BUCKETS))
        pos = (start + rank).reshape(-1, 1, tm)
        hi, lo = pos // SUBLANES, pos % SUBLANES
        addr.append((hi, lo))
        hs = _scatter(hi, lo, hx.reshape(-1, SUBLANES, w), hs, tm)
    ys = _experts(e1, e2, blk, valid, hs.reshape(-1, w), wg, wu, wd, tm_exp)
    ys = ys.reshape(-1, SUBLANES, ys.shape[-1])
    return [_gather_res(hi, lo, x, g2, ys, tm) for (x, g2, hx, bkt, rank, tm), (hi, lo) in zip(parts, addr)]


def _relayout_w_in(w_in, d_model):
    sizes = (MLA_Q_RANK, MLA_KV_RANK, MLA_ROPE, 2 * CONV_CH, GQA_HEADS * GQA_HD, GQA_KV_HEADS * GQA_HD,
             GQA_KV_HEADS * GQA_HD, N_BRANCH * d_model)
    offs = np.cumsum((0,) + sizes)
    cq, ckv, kpe, conv, gq, gk, gv, gates = [w_in[..., offs[i]:offs[i + 1]] for i in range(len(sizes))]
    lead = w_in.shape[:-1]
    z = lambda n: jnp.zeros(lead + (n,), w_in.dtype)
    head = lambda hh: gq[..., hh * GQA_HD:(hh + 1) * GQA_HD]
    gq_slabs = [head(hh) for p in range(GQA_HEADS // 2) for hh in (p, p + GQA_HEADS // 2)]
    out = jnp.concatenate([cq, ckv, z(HALF), kpe, z(HALF - MLA_ROPE), gk, gv] + gq_slabs + [conv, gates], axis=-1)
    return out.astype(BF16)


def _relayout_uq(w_uq):
    l, r, _ = w_uq.shape
    w = w_uq.reshape(l, r, MLA_HEADS, MLA_QK)
    w = jnp.pad(w, ((0, 0), (0, 0), (0, 0), (0, LANES - MLA_QK)))
    return w.reshape(l, r, MLA_HEADS * LANES).astype(BF16)


def _relayout_ukv(w_ukv):
    l, r, _ = w_ukv.shape
    w = w_ukv.reshape(l, r, MLA_HEADS, MLA_NOPE + MLA_V)
    k = jnp.pad(w[..., :MLA_NOPE], ((0, 0), (0, 0), (0, 0), (0, LANES - MLA_NOPE)))
    v = w[..., MLA_NOPE:]
    return jnp.concatenate([k.reshape(l, r, MLA_HEADS * LANES), v.reshape(l, r, MLA_HEADS * MLA_V)],
                           axis=-1).astype(BF16)


def _rope_tables(n, r):
    rows = n // GRID_W
    row = jnp.repeat(jnp.arange(rows, dtype=jnp.int32), GRID_W).astype(F32)
    col = jnp.tile(jnp.arange(GRID_W, dtype=jnp.int32), rows).astype(F32)
    dim = r // 2
    inv = ROPE_BASE ** (-jnp.arange(0, dim, 2, dtype=F32) / dim)
    ar, ac = row[:, None] * inv[None, :], col[:, None] * inv[None, :]
    zero = jnp.zeros_like(ar)
    cos = jnp.concatenate([jnp.cos(ar), jnp.cos(ar), jnp.cos(ac), jnp.cos(ac)], axis=-1)
    sa = jnp.concatenate([-jnp.sin(ar), zero, -jnp.sin(ac), zero], axis=-1)
    sb = jnp.concatenate([zero, jnp.sin(ar), zero, jnp.sin(ac)], axis=-1)
    return cos, sa, sb


def _all_tables(n):
    mc, msa, msb = _rope_tables(n, MLA_ROPE)
    one = jnp.ones((n, MLA_NOPE), F32)
    zero = jnp.zeros((n, MLA_NOPE), F32)
    pad1 = jnp.ones((n, LANES - MLA_QK), F32)
    pad0 = jnp.zeros((n, LANES - MLA_QK), F32)
    mla = (jnp.concatenate([one, mc, pad1], -1), jnp.concatenate([zero, msa, pad0], -1),
           jnp.concatenate([zero, msb, pad0], -1))
    gqa = tuple(jnp.concatenate([t, t], -1) for t in _rope_tables(n, GQA_HD))
    return mla + gqa


def kernel(x, c, ctx, c_ctx, w_mod, b_mod, norm1_g, norm2_g, w_in, mla_q_norm, mla_kv_norm, mla_w_uq, mla_w_ukv,
           mla_q_gain, mla_k_gain, mla_w_o, conv_w, conv_b, conv_ln_g, conv_ln_b, conv_w_o, gqa_q_gain, gqa_k_gain,
           gqa_w_o, w_out, router_w, router_bias, moe_w_gate, moe_w_up, moe_w_down):
    b, n, d = x.shape
    n_ctx = ctx.shape[1]
    depth = w_mod.shape[0]
    assert n % GRID_W == 0 and n % 512 == 0 and n_ctx % LANES == 0 and d % LANES == 0

    tm = 512
    tm_ctx = min(n_ctx, 256)
    tq = 512
    tm_exp = 512

    m_rows = -(-(b + 1) // 8) * 8
    c_all = jnp.concatenate([c, c_ctx[None, :], jnp.zeros((m_rows - b - 1, d), F32)], axis=0)
    mod = _modulation(c_all, w_mod, b_mod)

    w_in_r = _relayout_w_in(w_in, d)
    w_uq_r = _relayout_uq(mla_w_uq)
    w_ukv_r = _relayout_ukv(mla_w_ukv)
    zl = lambda k: jnp.zeros((depth, k), F32)
    qg128 = jnp.concatenate([mla_q_gain, zl(LANES - MLA_QK)], -1)
    kg128 = jnp.concatenate([mla_k_gain, zl(LANES - MLA_QK)], -1)
    row1 = jnp.concatenate([mla_q_norm, mla_kv_norm, qg128, kg128, gqa_q_gain, gqa_q_gain, gqa_k_gain, gqa_k_gain,
                            zl(d - V_GGK - LANES)], -1)
    vec = jnp.stack([norm1_g, row1], axis=1)
    gqa_perm = np.concatenate([np.arange(GQA_HD) + GQA_HD * hh for pair in GQA_PAIRS for hh in pair])
    wm_o = mla_w_o.astype(BF16)
    wg_o = gqa_w_o[:, gqa_perm, :].astype(BF16)
    wc_o = conv_w_o.astype(BF16)
    wo = w_out.astype(BF16)
    cw = jnp.pad(conv_w, ((0, 0), (0, 32 - CONV_W), (0, 0)))
    cvec = jnp.pad(jnp.stack([conv_b, conv_ln_g, conv_ln_b], axis=1), ((0, 0), (0, 5), (0, 0)))
    rwt = router_w.T
    rw_hi = rwt.astype(BF16)
    rw = jnp.stack([rw_hi, (rwt - rw_hi.astype(F32)).astype(BF16)])
    rb = router_bias.reshape(N_EXPERTS, 1)
    wg_e, wu_e, wd_e = moe_w_gate.astype(BF16), moe_w_up.astype(BF16), moe_w_down.astype(BF16)
    tables = _all_tables(n)

    xc = ctx
    for l in range(depth):
        last = l == depth - 1
        sh1, sc1, g1, sh2, sc2, g2 = [t[:, None, :] for t in jnp.split(mod[l, :b], 6, axis=-1)]
        csh1, csc1, cg1, csh2, csc2, cg2 = [jnp.broadcast_to(t[None, :, :], (b, 1, d))
                                            for t in jnp.split(mod[l, b:b + 1], 6, axis=-1)]
        qm, km, vm, qg, kg, vg, y, gates = _pre_mixer(x, sc1, sh1, vec[l], w_in_r[l], w_uq_r[l], w_ukv_r[l],
                                                      tables, tm)
        cqm, ckm, cvm, cqg, ckg, cvg, cy, cgates = _pre_mixer(xc, csc1, csh1, vec[l], w_in_r[l], w_uq_r[l],
                                                              w_ukv_r[l], None, tm_ctx)
        kg4, vg4, ckg4, cvg4 = kg[:, None], vg[:, None], ckg[:, None], cvg[:, None]
        o_mla = _attention(qm, [km, vm, ckm, cvm], MLA_HEAD_MAP, MLA_PAIRS, tq, "attn_mla")
        o_gqa = _attention(qg, [kg4, vg4, ckg4, cvg4], GQA_HEAD_MAP, GQA_PAIRS, tq, "attn_gqa")
        z = _conv(y, cw[l], cvec[l])
        n2 = norm2_g[l][None, :]
        out_w = (wm_o[l], wc_o[l], wg_o[l], wo[l])
        x, hx, bkt, rank, cnt = _merge_route(x, g1, o_mla, z, o_gqa, gates, *out_w, sc2, sh2, n2, rw, rb,
                                             jnp.zeros((BUCKET_ROWS, LANES), F32), tm)
        moe_parts = [(x, g2, hx, bkt, rank, tm)]
        if not last:
            co_mla = _attention(cqm, [ckm, cvm], MLA_HEAD_MAP, MLA_PAIRS, tm_ctx, "attn_mla_ctx")
            co_gqa = _attention(cqg, [ckg4, cvg4], GQA_HEAD_MAP, GQA_PAIRS, tm_ctx, "attn_gqa_ctx")
            cz = _conv(cy, cw[l], cvec[l])
            xc, chx, cbkt, crank, cnt = _merge_route(xc, cg1, co_mla, cz, co_gqa, cgates, *out_w, csc2, csh2, n2,
                                                     rw, rb, cnt, tm_ctx)
            moe_parts.append((xc, cg2, chx, cbkt, crank, tm_ctx))
        outs = _moe(moe_parts, cnt, wg_e[l], wu_e[l], wd_e[l], tm_exp)
        x = outs[0]
        if not last:
            xc = outs[1]
    return x
```

```python
import functools
import math

import numpy as np
import jax
import jax.numpy as jnp
from jax import lax
from jax.experimental import pallas as pl
from jax.experimental.pallas import tpu as pltpu

GRID_W = 64
ROPE_BASE = 10000.0
EPS = 1e-6

MLA_HEADS = 8
MLA_Q_RANK = 256
MLA_KV_RANK = 128
MLA_NOPE = 64
MLA_ROPE = 32
MLA_V = 64
MLA_QK = MLA_NOPE + MLA_ROPE
CONV_CH = 512
CONV_W = 31
GQA_HEADS = 8
GQA_KV_HEADS = 2
GQA_HD = 64
N_BRANCH = 3
N_EXPERTS = 16
N_GROUPS = 4
EXPERTS_PER_GROUP = 4
D_EXPERT = 512

LANES = 128
SUBLANES = 8
HALF = LANES // 2
N_PAIRS = 6
N_BUCKETS = N_GROUPS * N_PAIRS
BUCKET_ROWS = 32
LOG2E = math.log2(math.e)
VMEM_LIMIT = 56 * 1024 * 1024

C_CQ = 0
C_CKV = C_CQ + MLA_Q_RANK
C_KPE = C_CKV + MLA_KV_RANK
C_GK = C_KPE + LANES
C_GV = C_GK + LANES
C_GQ = C_GV + LANES
C_CA = C_GQ + GQA_HEADS * GQA_HD
C_CG = C_CA + CONV_CH
C_GATE = C_CG + CONV_CH
D_IN_R = C_GATE

V_QN = 0
V_KVN = 256
V_QG = 384
V_KG = 512
V_GGQ = 640
V_GGK = 768

BF16 = jnp.bfloat16
F32 = jnp.float32
NT_DIMS = (((1,), (1,)), ((), ()))


def _dot(a, b):
    return jnp.dot(a, b, preferred_element_type=F32)


def _sigmoid(v):
    return 0.5 * jnp.tanh(0.5 * v) + 0.5


def _split_hi_lo(a):
    hi = a.astype(BF16)
    lo = (a - hi.astype(F32)).astype(BF16)
    return hi, lo


def _params(*sem):
    return pltpu.CompilerParams(dimension_semantics=sem, vmem_limit_bytes=VMEM_LIMIT)


def _const_spec(shape):
    nd = len(shape)
    return pl.BlockSpec(shape, lambda *_: (0,) * nd)


def _mod_kernel(c_ref, w_ref, b_ref, o_ref):
    c = c_ref[...]
    s = c * _sigmoid(c)
    sh, sl = _split_hi_lo(s)
    wh, wl = _split_hi_lo(w_ref[0])
    o_ref[0] = _dot(sh, wh) + _dot(sh, wl) + _dot(sl, wh) + b_ref[0]


def _modulation(c_all, w_mod, b_mod):
    depth, d, d6 = w_mod.shape
    m = c_all.shape[0]
    tn = 1536
    return pl.pallas_call(
        _mod_kernel,
        grid=(depth, d6 // tn),
        in_specs=[pl.BlockSpec((m, d), lambda l, j: (0, 0)),
                  pl.BlockSpec((1, d, tn), lambda l, j: (l, 0, j)),
                  pl.BlockSpec((1, 1, tn), lambda l, j: (l, 0, j))],
        out_specs=pl.BlockSpec((1, m, tn), lambda l, j: (l, 0, j)),
        out_shape=jax.ShapeDtypeStruct((depth, m, d6), F32),
        compiler_params=_params("parallel", "parallel"),
        name="modulation",
    )(c_all, w_mod, b_mod.reshape(depth, 1, d6))


def _rope(n, cos, sa, sb, quarter):
    return n * cos + pltpu.roll(n, LANES - quarter, 1) * sa + pltpu.roll(n, quarter, 1) * sb


def _pre_mixer_kernel(x_ref, sc_ref, sh_ref, vec_ref, w_ref, wuq_ref, wukv_ref, *rest, use_rope, d_model, t_sub):
    for r0 in range(0, x_ref.shape[1], t_sub):
        _pre_mixer_rows(slice(r0, r0 + t_sub), x_ref, sc_ref, sh_ref, vec_ref, w_ref, wuq_ref, wukv_ref, rest,
                        use_rope, d_model)


def _pre_mixer_rows(rows, x_ref, sc_ref, sh_ref, vec_ref, w_ref, wuq_ref, wukv_ref, rest, use_rope, d_model):
    if use_rope:
        mc_ref, msa_ref, msb_ref, gc_ref, gsa_ref, gsb_ref = rest[:6]
        rest = rest[6:]
    qm_ref, km_ref, vm_ref, qg_ref, kg_ref, vg_ref, y_ref, gate_ref = rest

    x = x_ref[0, rows, :]
    inv = lax.rsqrt(jnp.mean(x * x, axis=-1, keepdims=True) + EPS)
    h = (x * inv * vec_ref[0:1, :]) * (1.0 + sc_ref[0]) + sh_ref[0]
    hb = h.astype(BF16)

    lane = lax.broadcasted_iota(jnp.int32, (1, LANES), 1)
    lo = lane < HALF

    def proj(c0, width):
        return _dot(hb, w_ref[:, c0:c0 + width])

    def rms(v, g):
        return v * lax.rsqrt(jnp.mean(v * v, axis=-1, keepdims=True) + EPS) * g

    def seg_inv(xs, n_lo, n_hi):
        sq = xs * xs
        if n_hi is None:
            return lax.rsqrt(jnp.sum(sq, axis=-1, keepdims=True) * (1.0 / n_lo) + EPS)
        s_lo = jnp.sum(jnp.where(lo, sq, 0.0), axis=-1, keepdims=True)
        s_hi = jnp.sum(jnp.where(lo, 0.0, sq), axis=-1, keepdims=True)
        return jnp.where(lo, lax.rsqrt(s_lo * (1.0 / n_lo) + EPS), lax.rsqrt(s_hi * (1.0 / n_hi) + EPS))

    def mla_rope(n):
        return _rope(n, mc_ref[rows, :], msa_ref[rows, :], msb_ref[rows, :], MLA_ROPE // 4) if use_rope else n

    def gqa_rope(n):
        return _rope(n, gc_ref[rows, :], gsa_ref[rows, :], gsb_ref[rows, :], GQA_HD // 4) if use_rope else n

    small = proj(0, C_CA)

    def narrow(c0, width):
        return small[:, c0:c0 + width]

    cqn = rms(narrow(C_CQ, MLA_Q_RANK), vec_ref[1:2, V_QN:V_QN + MLA_Q_RANK])
    q = _dot(cqn.astype(BF16), wuq_ref[...])
    qgain = vec_ref[1:2, V_QG:V_QG + LANES]
    for hh in range(MLA_HEADS):
        xs = q[:, hh * LANES:(hh + 1) * LANES]
        n = mla_rope(xs * seg_inv(xs, MLA_NOPE, MLA_ROPE) * qgain)
        qm_ref[0, hh, rows, :] = (n * (MLA_QK ** -0.5 * LOG2E)).astype(BF16)

    kgain = vec_ref[1:2, V_KG:V_KG + LANES]
    kpe = narrow(C_KPE, LANES)
    kpe = mla_rope(kpe * seg_inv(kpe, MLA_ROPE, None) * kgain)
    ckvn = rms(narrow(C_CKV, MLA_KV_RANK), vec_ref[1:2, V_KVN:V_KVN + MLA_KV_RANK])
    kv = _dot(ckvn.astype(BF16), wukv_ref[...])
    for hh in range(MLA_HEADS):
        xs = kv[:, hh * LANES:(hh + 1) * LANES]
        km_ref[0, hh, rows, :] = (xs * seg_inv(xs, MLA_NOPE, None) * kgain + kpe).astype(BF16)
    ones_slab = jnp.ones((x.shape[0], LANES), BF16)
    v0 = MLA_HEADS * LANES
    for p in range(MLA_HEADS // 2):
        pair = kv[:, v0 + p * LANES:v0 + (p + 1) * LANES].astype(BF16)
        vm_ref[0, 2 * p, rows, :] = jnp.where(lo, pair, ones_slab)
        vm_ref[0, 2 * p + 1, rows, :] = jnp.where(lo, ones_slab, pair)

    gk = narrow(C_GK, LANES)
    ggk = vec_ref[1:2, V_GGK:V_GGK + LANES]
    kg_ref[0, rows, :] = gqa_rope(gk * seg_inv(gk, GQA_HD, GQA_HD) * ggk).astype(BF16)
    gv = narrow(C_GV, LANES).astype(BF16)
    vg_ref[0, 0, rows, :] = jnp.where(lo, gv, ones_slab)
    vg_ref[0, 1, rows, :] = jnp.where(lo, ones_slab, gv)
    ggq = vec_ref[1:2, V_GGQ:V_GGQ + LANES]
    for p in range(GQA_HEADS // 2):
        xs = narrow(C_GQ + p * LANES, LANES)
        n = gqa_rope(xs * seg_inv(xs, GQA_HD, GQA_HD) * ggq)
        qg_ref[0, p, rows, :] = (n * (GQA_HD ** -0.5 * LOG2E)).astype(BF16)

    y_ref[0, rows, :] = proj(C_CA, CONV_CH) * _sigmoid(proj(C_CG, CONV_CH))
    for j in range(N_BRANCH):
        gate_ref[0, rows, j * d_model:(j + 1) * d_model] = _sigmoid(
            proj(C_GATE + j * d_model, d_model)).astype(BF16)


PRE_SUB = 512


def _pre_mixer(x, sc, sh, vec, w_in_r, w_uq_r, w_ukv_r, tables, tm):
    b, n, d = x.shape
    use_rope = tables is not None
    nt = n // tm
    bmap = lambda i, j: (j, 0, 0)
    in_specs = [pl.BlockSpec((1, tm, d), lambda i, j: (j, i, 0)),
                pl.BlockSpec((1, 1, d), bmap),
                pl.BlockSpec((1, 1, d), bmap),
                _const_spec(vec.shape), _const_spec(w_in_r.shape), _const_spec(w_uq_r.shape),
                _const_spec(w_ukv_r.shape)]
    args = [x, sc, sh, vec, w_in_r, w_uq_r, w_ukv_r]
    if use_rope:
        in_specs += [pl.BlockSpec((tm, LANES), lambda i, j: (i, 0))] * 6
        args += list(tables)
    hspec = lambda nh, w=LANES: pl.BlockSpec((1, nh, tm, w), lambda i, j: (j, 0, i, 0))
    tspec = lambda w: pl.BlockSpec((1, tm, w), lambda i, j: (j, i, 0))
    out_specs = [hspec(MLA_HEADS), hspec(MLA_HEADS), hspec(MLA_HEADS), hspec(GQA_HEADS // 2),
                 tspec(LANES), hspec(GQA_KV_HEADS), tspec(CONV_CH), tspec(N_BRANCH * d)]
    sds = jax.ShapeDtypeStruct
    out_shape = [sds((b, MLA_HEADS, n, LANES), BF16), sds((b, MLA_HEADS, n, LANES), BF16),
                 sds((b, MLA_HEADS, n, LANES), BF16), sds((b, GQA_HEADS // 2, n, LANES), BF16),
                 sds((b, n, LANES), BF16), sds((b, GQA_KV_HEADS, n, LANES), BF16),
                 sds((b, n, CONV_CH), F32), sds((b, n, N_BRANCH * d), BF16)]
    return pl.pallas_call(
        functools.partial(_pre_mixer_kernel, use_rope=use_rope, d_model=d, t_sub=min(tm, PRE_SUB)),
        grid=(nt, b), in_specs=in_specs, out_specs=out_specs, out_shape=out_shape,
        compiler_params=_params("parallel", "parallel"),
        name="pre_mixer_rope" if use_rope else "pre_mixer",
    )(*args)


def _attn_kernel(q_ref, *refs, n_seg, head_map, pairs, t_sub, t_piece):
    kv_refs = refs[:2 * n_seg]
    o_ref = refs[2 * n_seg]
    lane = lax.broadcasted_iota(jnp.int32, (1, LANES), 1)
    lo = lane < HALF

    def one_head(hh, rows, out_half):
        qidx, q_half, kidx, vidx = head_map[hh]
        sum_lane = HALF if out_half == 0 else 0
        q = q_ref[0, qidx, rows, :]
        if q_half is not None:
            q = jnp.where(lo if q_half == 0 else jnp.logical_not(lo), q, jnp.zeros_like(q))
        ss_all = [lax.dot_general(q, kv_refs[2 * s][0, kidx], NT_DIMS, preferred_element_type=F32)
                  for s in range(n_seg)]
        outs = []
        for p0 in range(0, ss_all[0].shape[0], t_piece):
            ss = [s[p0:p0 + t_piece] for s in ss_all]
            m = functools.reduce(jnp.maximum, [jnp.max(s, axis=-1, keepdims=True) for s in ss])
            ol = functools.reduce(jnp.add, [_dot(jnp.exp2(ss[s] - m).astype(BF16), kv_refs[2 * s + 1][0, vidx])
                                            for s in range(n_seg)])
            outs.append(ol / ol[:, sum_lane:sum_lane + 1])
        return outs[0] if len(outs) == 1 else jnp.concatenate(outs, axis=0)

    for r0 in range(0, q_ref.shape[2], t_sub):
        rows = slice(r0, r0 + t_sub)
        for p, (h_lo, h_hi) in enumerate(pairs):
            o_ref[0, rows, p * LANES:(p + 1) * LANES] = jnp.where(
                lo, one_head(h_lo, rows, 0), one_head(h_hi, rows, 1)).astype(BF16)


MLA_HEAD_MAP = tuple((h, None, h, h) for h in range(MLA_HEADS))
MLA_PAIRS = tuple((2 * p, 2 * p + 1) for p in range(MLA_HEADS // 2))
GQA_HEAD_MAP = tuple((h % (GQA_HEADS // 2), h // (GQA_HEADS // 2), 0, h // (GQA_HEADS // 2))
                     for h in range(GQA_HEADS))
GQA_PAIRS = tuple((a, a + GQA_HEADS // 2) for a in range(GQA_HEADS // 2))


ATTN_SUB = 512
ATTN_PIECE = 256


def _attention(q, kvs, head_map, pairs, tq, name):
    b, nh, n, _ = q.shape
    in_specs = [pl.BlockSpec((1, nh, tq, LANES), lambda i, j: (i, 0, j, 0))]
    args = [q]
    for a in kvs:
        in_specs.append(pl.BlockSpec((1,) + a.shape[1:], lambda i, j: (i, 0, 0, 0)))
        args.append(a)
    width = len(pairs) * LANES
    return pl.pallas_call(
        functools.partial(_attn_kernel, n_seg=len(kvs) // 2, head_map=head_map, pairs=pairs,
                          t_sub=min(tq, ATTN_SUB), t_piece=min(tq, ATTN_PIECE)),
        grid=(b, n // tq), in_specs=in_specs,
        out_specs=pl.BlockSpec((1, tq, width), lambda i, j: (i, j, 0)),
        out_shape=jax.ShapeDtypeStruct((b, n, width), BF16),
        compiler_params=_params("parallel", "parallel"),
        name=name,
    )(*args)


CONV_PAD = 16
CONV_TAIL = 24
CONV_CHUNK = 256


def _conv_kernel(y_ref, cw_ref, vec_ref, z_ref, buf_ref, ph_ref, *, n):
    ch = y_ref.shape[-1]
    buf_ref[0:CONV_PAD, :] = jnp.zeros((CONV_PAD, ch), F32)
    buf_ref[CONV_PAD + n:CONV_PAD + n + CONV_TAIL, :] = jnp.zeros((CONV_TAIL, ch), F32)
    buf_ref[CONV_PAD:CONV_PAD + n, :] = y_ref[0]
    first = CONV_PAD - CONV_W // 2
    rows = ph_ref.shape[1] - CONV_TAIL
    win_rows = rows + CONV_W + 1

    def chunk(c, carry):
        r0 = pl.multiple_of(c * rows, rows)
        win = buf_ref[pl.ds(r0, win_rows), :]
        for s in range(SUBLANES):
            ph_ref[s] = win[s:s + ph_ref.shape[1], :]
        acc = jnp.zeros((rows, ch), F32) + vec_ref[0:1, :]
        for k in range(CONV_W):
            a, s = divmod(first + k, SUBLANES)
            acc = acc + ph_ref[s, a * SUBLANES:a * SUBLANES + rows, :] * cw_ref[k:k + 1, :]
        mu = jnp.mean(acc, axis=-1, keepdims=True)
        cen = acc - mu
        var = jnp.mean(cen * cen, axis=-1, keepdims=True)
        yn = cen * lax.rsqrt(var + EPS) * vec_ref[1:2, :] + vec_ref[2:3, :]
        z_ref[0, pl.ds(r0, rows), :] = (yn * _sigmoid(yn)).astype(BF16)
        return carry

    lax.fori_loop(0, n // rows, chunk, 0)


def _conv(y, cw, cvec):
    b, n, ch = y.shape
    chunk = min(CONV_CHUNK, n)
    assert n % chunk == 0 and chunk % SUBLANES == 0
    return pl.pallas_call(
        functools.partial(_conv_kernel, n=n),
        grid=(b,),
        in_specs=[pl.BlockSpec((1, n, ch), lambda i: (i, 0, 0)), _const_spec(cw.shape), _const_spec(cvec.shape)],
        out_specs=pl.BlockSpec((1, n, ch), lambda i: (i, 0, 0)),
        out_shape=jax.ShapeDtypeStruct((b, n, ch), BF16),
        scratch_shapes=[pltpu.VMEM((CONV_PAD + n + CONV_TAIL, ch), F32),
                        pltpu.VMEM((SUBLANES, chunk + CONV_TAIL, ch), F32)],
        compiler_params=_params("parallel"),
        name="conv",
    )(y, cw, cvec)


def _merge_route_kernel(x_ref, g_ref, om_ref, oc_ref, og_ref, gate_ref, wm_ref, wc_ref, wg_ref, wo_ref,
                        sc_ref, sh_ref, n2_ref, rw_ref, rb_ref, cin_ref,
                        o_ref, hx_ref, bkt_ref, rank_ref, cnt_ref, *, d_model):
    d = d_model
    merged = (gate_ref[0, :, 0:d].astype(F32) * _dot(om_ref[0], wm_ref[...])
              + gate_ref[0, :, d:2 * d].astype(F32) * _dot(oc_ref[0], wc_ref[...])
              + gate_ref[0, :, 2 * d:3 * d].astype(F32) * _dot(og_ref[0], wg_ref[...]))
    x_new = x_ref[0] + g_ref[0] * _dot(merged.astype(BF16), wo_ref[...])
    o_ref[0] = x_new
    _route(x_new, sc_ref, sh_ref, n2_ref, rw_ref, rb_ref, cin_ref, hx_ref, bkt_ref, rank_ref, cnt_ref, d)


def _merge_route(x, g1, o_mla, z, o_gqa, gates, wm, wc, wg, wo, sc2, sh2, n2, rw, rb, cin, tm):
    b, n, d = x.shape
    nt = n // tm
    sds = jax.ShapeDtypeStruct
    tok = lambda w: pl.BlockSpec((1, tm, w), lambda i, j: (i, j, 0))
    per_b = pl.BlockSpec((1, 1, d), lambda i, j: (i, 0, 0))
    ispec = pl.BlockSpec((1, 1, 1, tm), lambda i, j: (i, j, 0, 0))
    return pl.pallas_call(
        functools.partial(_merge_route_kernel, d_model=d),
        grid=(b, nt),
        in_specs=[tok(d), per_b, tok(o_mla.shape[-1]), tok(z.shape[-1]), tok(o_gqa.shape[-1]), tok(N_BRANCH * d),
                  _const_spec(wm.shape), _const_spec(wc.shape), _const_spec(wg.shape), _const_spec(wo.shape),
                  per_b, per_b, _const_spec(n2.shape), _const_spec(rw.shape), _const_spec(rb.shape),
                  _const_spec(cin.shape)],
        out_specs=[tok(d), tok(d + LANES), ispec, ispec, _const_spec((BUCKET_ROWS, LANES))],
        out_shape=[sds((b, n, d), F32), sds((b, n, d + LANES), F32), sds((b, nt, 1, tm), jnp.int32),
                   sds((b, nt, 1, tm), jnp.int32), sds((BUCKET_ROWS, LANES), F32)],
        compiler_params=_params("arbitrary", "arbitrary"),
        name="merge_route",
    )(x, g1, o_mla, z, o_gqa, gates, wm, wc, wg, wo, sc2, sh2, n2, rw, rb, cin)


def _first_argmax4(v):
    m = functools.reduce(jnp.maximum, v)
    idx = jnp.where(v[0] == m, 0, jnp.where(v[1] == m, 1, jnp.where(v[2] == m, 2, 3)))
    return m, idx


def _pick4(idx, v):
    return jnp.where(idx == 0, v[0], jnp.where(idx == 1, v[1], jnp.where(idx == 2, v[2], v[3])))


def _route(x, sc_ref, sh_ref, g_ref, rw_ref, rb_ref, cin_ref, hx_ref, bkt_ref, rank_ref, cnt_ref, d_model):
    tm = x.shape[0]
    first = (pl.program_id(0) == 0) & (pl.program_id(1) == 0)

    @pl.when(first)
    def _():
        cnt_ref[...] = cin_ref[...]

    inv = lax.rsqrt(jnp.mean(x * x, axis=-1, keepdims=True) + EPS)
    h = (x * inv * g_ref[...]) * (1.0 + sc_ref[0]) + sh_ref[0]
    hx_ref[0, :, 0:d_model] = h

    hh, hl = _split_hi_lo(h)
    rwh, rwl = rw_ref[0], rw_ref[1]
    logits = (lax.dot_general(rwh, hh, NT_DIMS, preferred_element_type=F32)
              + lax.dot_general(rwh, hl, NT_DIMS, preferred_element_type=F32)
              + lax.dot_general(rwl, hh, NT_DIMS, preferred_element_type=F32))
    s_all = _sigmoid(logits)
    sel_all = s_all + rb_ref[...]
    s = [s_all[e:e + 1, :] for e in range(N_EXPERTS)]
    sel = [sel_all[e:e + 1, :] for e in range(N_EXPERTS)]

    gscore = []
    for g in range(N_GROUPS):
        a, b, c, d = sel[4 * g:4 * g + 4]
        gscore.append(functools.reduce(jnp.maximum, [a + b, a + c, a + d, b + c, b + d, c + d]))
    _, gi = _first_argmax4(gscore)
    v = [_pick4(gi, [sel[4 * g + j] for g in range(N_GROUPS)]) for j in range(EXPERTS_PER_GROUP)]
    u = [_pick4(gi, [s[4 * g + j] for g in range(N_GROUPS)]) for j in range(EXPERTS_PER_GROUP)]
    _, i1 = _first_argmax4(v)
    _, i2 = _first_argmax4([jnp.where(i1 == j, -jnp.inf, v[j]) for j in range(EXPERTS_PER_GROUP)])
    e_lo = jnp.minimum(i1, i2)
    e_hi = jnp.maximum(i1, i2)
    pair = jnp.where(e_lo == 0, e_hi - 1, jnp.where(e_lo == 1, e_hi + 1, N_PAIRS - 1))
    bucket = gi * N_PAIRS + pair
    u_lo = _pick4(e_lo, u)
    u_hi = _pick4(e_hi, u)
    den = u_lo + u_hi
    wts = jnp.concatenate([u_lo / den, u_hi / den, jnp.zeros((LANES - 2, tm), F32)], axis=0)
    hx_ref[0, :, d_model:d_model + LANES] = wts.T

    onehot = lax.broadcasted_iota(jnp.int32, (BUCKET_ROWS, tm), 0) == bucket
    oh = jnp.where(onehot, 1.0, 0.0)
    before = lax.broadcasted_iota(jnp.int32, (tm, tm), 0) < lax.broadcasted_iota(jnp.int32, (tm, tm), 1)
    cum = _dot(oh.astype(BF16), jnp.where(before, 1.0, 0.0).astype(BF16))
    rank = jnp.sum(oh * (cum + cnt_ref[:, 0:1]), axis=0, keepdims=True)
    bkt_ref[0, 0] = bucket
    rank_ref[0, 0] = rank.astype(jnp.int32)
    cnt_ref[...] = cnt_ref[...] + jnp.sum(oh, axis=1, keepdims=True)


ROW_UNROLL = 2
N_DMA_PRIORITIES = 2


def _issue_rows(tm, row_copy, hi_ref, lo_ref):
    def issue(i, c):
        for j in range(SUBLANES):
            r = i * SUBLANES + j
            lo = lo_ref[0, 0, r] & (SUBLANES - 1)
            row_copy(i, j, hi_ref[0, 0, r], lo).start(priority=j % N_DMA_PRIORITIES)
        return c

    lax.fori_loop(0, tm // SUBLANES, issue, 0, unroll=ROW_UNROLL)


def _drain_rows(tm, row_copy):
    def drain(r, c):
        row_copy(0, 0, 0, 0).wait()
        return c

    lax.fori_loop(0, tm, drain, 0, unroll=SUBLANES)


def _scatter_kernel(hi_ref, lo_ref, src_ref, hs_in_ref, hs_ref, sem):
    del hs_in_ref
    tm = src_ref.shape[0] * SUBLANES

    def row_copy(i, j, hi, lo):
        return pltpu.make_async_copy(src_ref.at[i, pl.ds(j, 1)], hs_ref.at[hi, pl.ds(lo, 1)], sem)

    _issue_rows(tm, row_copy, hi_ref, lo_ref)
    _drain_rows(tm, row_copy)


def _scatter(hi3, lo3, src, hs, tm):
    t8, _, w = src.shape
    ispec = pl.BlockSpec((1, 1, tm), lambda i: (i, 0, 0), memory_space=pltpu.SMEM)
    return pl.pallas_call(
        _scatter_kernel,
        grid=(t8 * SUBLANES // tm,),
        in_specs=[ispec, ispec, pl.BlockSpec((tm // SUBLANES, SUBLANES, w), lambda i: (i, 0, 0)),
                  pl.BlockSpec(memory_space=pl.ANY)],
        out_specs=pl.BlockSpec(memory_space=pl.ANY),
        out_shape=jax.ShapeDtypeStruct(hs.shape, hs.dtype),
        scratch_shapes=[pltpu.SemaphoreType.DMA(())],
        input_output_aliases={3: 0},
        compiler_params=_params("arbitrary"),
        name="scatter_rows",
    )(hi3, lo3, src, hs)


def _gather_res_kernel(hic_ref, loc_ref, hin_ref, lon_ref, x_ref, g_ref, ys_ref, o_ref, buf_ref, sems):
    step = pl.program_id(0)
    tm = buf_ref.shape[1] * SUBLANES

    def row_copy_on(slot):
        def row_copy(i, j, hi, lo):
            return pltpu.make_async_copy(ys_ref.at[hi, pl.ds(lo, 1)], buf_ref.at[slot, i, pl.ds(j, 1)],
                                         sems.at[slot])
        return row_copy

    slot = step % 2

    @pl.when(step == 0)
    def _():
        _issue_rows(tm, row_copy_on(slot), hic_ref, loc_ref)

    @pl.when(step + 1 < pl.num_programs(0))
    def _():
        _issue_rows(tm, row_copy_on(1 - slot), hin_ref, lon_ref)

    _drain_rows(tm, row_copy_on(slot))
    o_ref[0] = x_ref[0] + g_ref[0] * buf_ref[slot]


def _gather_res(hi3, lo3, x, g2, ys, tm):
    b, n, d = x.shape
    nt = n // tm
    steps = b * nt
    x4 = x.reshape(b, n // SUBLANES, SUBLANES, d)
    cur = pl.BlockSpec((1, 1, tm), lambda i: (i, 0, 0), memory_space=pltpu.SMEM)
    nxt = pl.BlockSpec((1, 1, tm), lambda i: (jnp.minimum(i + 1, steps - 1), 0, 0), memory_space=pltpu.SMEM)
    tok = pl.BlockSpec((1, tm // SUBLANES, SUBLANES, d), lambda i: (i // nt, i % nt, 0, 0))
    out = pl.pallas_call(
        _gather_res_kernel,
        grid=(steps,),
        in_specs=[cur, cur, nxt, nxt, tok, pl.BlockSpec((1, 1, 1, d), lambda i: (i // nt, 0, 0, 0)),
                  pl.BlockSpec(memory_space=pl.ANY)],
        out_specs=tok,
        out_shape=jax.ShapeDtypeStruct(x4.shape, F32),
        scratch_shapes=[pltpu.VMEM((2, tm // SUBLANES, SUBLANES, d), F32), pltpu.SemaphoreType.DMA((2,))],
        compiler_params=_params("arbitrary"),
        name="gather_residual",
    )(hi3, lo3, hi3, lo3, x4, g2[:, None], ys)
    return out.reshape(b, n, d)


def _experts_kernel(e1_ref, e2_ref, blk_ref, valid_ref, hs_ref, wg1_ref, wu1_ref, wd1_ref, wg2_ref, wu2_ref,
                    wd2_ref, o_ref, *, d_model):
    valid = valid_ref[pl.program_id(0)] > 0

    @pl.when(jnp.logical_not(valid))
    def _():
        o_ref[...] = jnp.zeros(o_ref.shape, F32)

    @pl.when(valid)
    def _():
        xb = hs_ref[:, 0:d_model].astype(BF16)

        def half(wg_ref, wu_ref, w):
            gte = _dot(xb, wg_ref[0])
            return (gte * _sigmoid(gte) * _dot(xb, wu_ref[0]) * w).astype(BF16)

        h1 = half(wg1_ref, wu1_ref, hs_ref[:, d_model:d_model + 1])
        h2 = half(wg2_ref, wu2_ref, hs_ref[:, d_model + 1:d_model + 2])
        o_ref[...] = _dot(h1, wd1_ref[0]) + _dot(h2, wd2_ref[0])


def _experts(e1, e2, blk, valid, hs, wg, wu, wd, tm):
    tp, w = hs.shape
    d = w - LANES
    f = wg.shape[-1]
    up1 = pl.BlockSpec((1, d, f), lambda i, e1, e2, blk, valid: (e1[i], 0, 0))
    up2 = pl.BlockSpec((1, d, f), lambda i, e1, e2, blk, valid: (e2[i], 0, 0))
    dn1 = pl.BlockSpec((1, f, d), lambda i, e1, e2, blk, valid: (e1[i], 0, 0))
    dn2 = pl.BlockSpec((1, f, d), lambda i, e1, e2, blk, valid: (e2[i], 0, 0))
    grid_spec = pltpu.PrefetchScalarGridSpec(
        num_scalar_prefetch=4, grid=(tp // tm,),
        in_specs=[pl.BlockSpec((tm, w), lambda i, e1, e2, blk, valid: (blk[i], 0)), up1, up1, dn1, up2, up2, dn2],
        out_specs=pl.BlockSpec((tm, d), lambda i, e1, e2, blk, valid: (i, 0)))
    return pl.pallas_call(
        functools.partial(_experts_kernel, d_model=d),
        grid_spec=grid_spec,
        out_shape=jax.ShapeDtypeStruct((tp, d), F32),
        compiler_params=_params("arbitrary"),
        name="moe_experts",
    )(e1, e2, blk, valid, hs, wg, wu, wd, wg, wu, wd)


PAIR_LO = np.array([0, 0, 0, 1, 1, 2], np.int32)
PAIR_HI = np.array([1, 2, 3, 2, 3, 3], np.int32)


def _moe_plan(counts, n_tiles, tm):
    c = counts[:N_BUCKETS]
    nt = (c + tm - 1) // tm
    tile_end = jnp.cumsum(nt)
    row_off = (tile_end - nt) * tm
    n_valid = tile_end[-1]
    i = jnp.arange(n_tiles, dtype=jnp.int32)
    blk = jnp.minimum(i, n_valid - 1)
    bkt = jnp.minimum(jnp.sum(blk[:, None] >= tile_end[None, :], axis=1), N_BUCKETS - 1).astype(jnp.int32)
    grp = bkt // N_PAIRS
    e1 = grp * EXPERTS_PER_GROUP + jnp.asarray(PAIR_LO)[bkt % N_PAIRS]
    e2 = grp * EXPERTS_PER_GROUP + jnp.asarray(PAIR_HI)[bkt % N_PAIRS]
    return row_off, e1, e2, blk.astype(jnp.int32), (i < n_valid).astype(jnp.int32)


def _sorted_rows_buffer(n_tokens, w, tm_exp):
    n_tiles = -(-n_tokens // tm_exp) + N_BUCKETS
    return jnp.zeros((n_tiles * tm_exp // SUBLANES, SUBLANES, w), F32)


def _moe(parts, cnt, hs, wg, wu, wd, tm_exp):
    n_tiles = hs.shape[0] * SUBLANES // tm_exp
    row_off, e1, e2, blk, valid = _moe_plan(cnt[:, 0].astype(jnp.int32), n_tiles, tm_exp)
    w = hs.shape[-1]
    addr = []
    for x, g2, hx, bkt, rank, tm in parts:
        start = sum(jnp.where(bkt == k, row_off[k], 0) for k in range(N_BUCKETS))
        pos = (start + rank).reshape(-1, 1, tm)
        hi, lo = pos // SUBLANES, pos % SUBLANES
        addr.append((hi, lo))
        hs = _scatter(hi, lo, hx.reshape(-1, SUBLANES, w), hs, tm)
    ys = _experts(e1, e2, blk, valid, hs.reshape(-1, w), wg, wu, wd, tm_exp)
    ys = ys.reshape(-1, SUBLANES, ys.shape[-1])
    outs = [_gather_res(hi, lo, x, g2, ys, tm) for (x, g2, hx, bkt, rank, tm), (hi, lo) in zip(parts, addr)]
    return outs, hs


def _relayout_w_in(w_in, d_model):
    sizes = (MLA_Q_RANK, MLA_KV_RANK, MLA_ROPE, 2 * CONV_CH, GQA_HEADS * GQA_HD, GQA_KV_HEADS * GQA_HD,
             GQA_KV_HEADS * GQA_HD, N_BRANCH * d_model)
    offs = np.cumsum((0,) + sizes)
    cq, ckv, kpe, conv, gq, gk, gv, gates = [w_in[..., offs[i]:offs[i + 1]] for i in range(len(sizes))]
    lead = w_in.shape[:-1]
    z = lambda n: jnp.zeros(lead + (n,), w_in.dtype)
    head = lambda hh: gq[..., hh * GQA_HD:(hh + 1) * GQA_HD]
    gq_slabs = [head(hh) for p in range(GQA_HEADS // 2) for hh in (p, p + GQA_HEADS // 2)]
    out = jnp.concatenate([cq, ckv, z(HALF), kpe, z(HALF - MLA_ROPE), gk, gv] + gq_slabs + [conv, gates], axis=-1)
    return out.astype(BF16)


def _relayout_uq(w_uq):
    l, r, _ = w_uq.shape
    w = w_uq.reshape(l, r, MLA_HEADS, MLA_QK)
    w = jnp.pad(w, ((0, 0), (0, 0), (0, 0), (0, LANES - MLA_QK)))
    return w.reshape(l, r, MLA_HEADS * LANES).astype(BF16)


def _relayout_ukv(w_ukv):
    l, r, _ = w_ukv.shape
    w = w_ukv.reshape(l, r, MLA_HEADS, MLA_NOPE + MLA_V)
    k = jnp.pad(w[..., :MLA_NOPE], ((0, 0), (0, 0), (0, 0), (0, LANES - MLA_NOPE)))
    v = w[..., MLA_NOPE:]
    return jnp.concatenate([k.reshape(l, r, MLA_HEADS * LANES), v.reshape(l, r, MLA_HEADS * MLA_V)],
                           axis=-1).astype(BF16)


def _rope_tables(n, r):
    rows = n // GRID_W
    row = jnp.repeat(jnp.arange(rows, dtype=jnp.int32), GRID_W).astype(F32)
    col = jnp.tile(jnp.arange(GRID_W, dtype=jnp.int32), rows).astype(F32)
    dim = r // 2
    inv = ROPE_BASE ** (-jnp.arange(0, dim, 2, dtype=F32) / dim)
    ar, ac = row[:, None] * inv[None, :], col[:, None] * inv[None, :]
    zero = jnp.zeros_like(ar)
    cos = jnp.concatenate([jnp.cos(ar), jnp.cos(ar), jnp.cos(ac), jnp.cos(ac)], axis=-1)
    sa = jnp.concatenate([-jnp.sin(ar), zero, -jnp.sin(ac), zero], axis=-1)
    sb = jnp.concatenate([zero, jnp.sin(ar), zero, jnp.sin(ac)], axis=-1)
    return cos, sa, sb


def _all_tables(n):
    mc, msa, msb = _rope_tables(n, MLA_ROPE)
    one = jnp.ones((n, MLA_NOPE), F32)
    zero = jnp.zeros((n, MLA_NOPE), F32)
    pad1 = jnp.ones((n, LANES - MLA_QK), F32)
    pad0 = jnp.zeros((n, LANES - MLA_QK), F32)
    mla = (jnp.concatenate([one, mc, pad1], -1), jnp.concatenate([zero, msa, pad0], -1),
           jnp.concatenate([zero, msb, pad0], -1))
    gqa = tuple(jnp.concatenate([t, t], -1) for t in _rope_tables(n, GQA_HD))
    return mla + gqa


def kernel(x, c, ctx, c_ctx, w_mod, b_mod, norm1_g, norm2_g, w_in, mla_q_norm, mla_kv_norm, mla_w_uq, mla_w_ukv,
           mla_q_gain, mla_k_gain, mla_w_o, conv_w, conv_b, conv_ln_g, conv_ln_b, conv_w_o, gqa_q_gain, gqa_k_gain,
           gqa_w_o, w_out, router_w, router_bias, moe_w_gate, moe_w_up, moe_w_down):
    b, n, d = x.shape
    n_ctx = ctx.shape[1]
    depth = w_mod.shape[0]
    assert n % GRID_W == 0 and n % 512 == 0 and n_ctx % LANES == 0 and d % LANES == 0

    tm = 512
    tm_ctx = min(n_ctx, 256)
    tq = 512
    tm_exp = 512

    m_rows = -(-(b + 1) // 8) * 8
    c_all = jnp.concatenate([c, c_ctx[None, :], jnp.zeros((m_rows - b - 1, d), F32)], axis=0)
    mod = _modulation(c_all, w_mod, b_mod)

    w_in_r = _relayout_w_in(w_in, d)
    w_uq_r = _relayout_uq(mla_w_uq)
    w_ukv_r = _relayout_ukv(mla_w_ukv)
    zl = lambda k: jnp.zeros((depth, k), F32)
    qg128 = jnp.concatenate([mla_q_gain, zl(LANES - MLA_QK)], -1)
    kg128 = jnp.concatenate([mla_k_gain, zl(LANES - MLA_QK)], -1)
    row1 = jnp.concatenate([mla_q_norm, mla_kv_norm, qg128, kg128, gqa_q_gain, gqa_q_gain, gqa_k_gain, gqa_k_gain,
                            zl(d - V_GGK - LANES)], -1)
    vec = jnp.stack([norm1_g, row1], axis=1)
    gqa_perm = np.concatenate([np.arange(GQA_HD) + GQA_HD * hh for pair in GQA_PAIRS for hh in pair])
    wm_o = mla_w_o.astype(BF16)
    wg_o = gqa_w_o[:, gqa_perm, :].astype(BF16)
    wc_o = conv_w_o.astype(BF16)
    wo = w_out.astype(BF16)
    cw = jnp.pad(conv_w, ((0, 0), (0, 32 - CONV_W), (0, 0)))
    cvec = jnp.pad(jnp.stack([conv_b, conv_ln_g, conv_ln_b], axis=1), ((0, 0), (0, 5), (0, 0)))
    rwt = router_w.T
    rw_hi = rwt.astype(BF16)
    rw = jnp.stack([rw_hi, (rwt - rw_hi.astype(F32)).astype(BF16)])
    rb = router_bias.reshape(N_EXPERTS, 1)
    wg_e, wu_e, wd_e = moe_w_gate.astype(BF16), moe_w_up.astype(BF16), moe_w_down.astype(BF16)
    tables = _all_tables(n)

    xc = ctx
    hs_buf = _sorted_rows_buffer(b * (n + n_ctx), d + LANES, tm_exp)
    for l in range(depth):
        last = l == depth - 1
        sh1, sc1, g1, sh2, sc2, g2 = [t[:, None, :] for t in jnp.split(mod[l, :b], 6, axis=-1)]
        csh1, csc1, cg1, csh2, csc2, cg2 = [jnp.broadcast_to(t[None, :, :], (b, 1, d))
                                            for t in jnp.split(mod[l, b:b + 1], 6, axis=-1)]
        qm, km, vm, qg, kg, vg, y, gates = _pre_mixer(x, sc1, sh1, vec[l], w_in_r[l], w_uq_r[l], w_ukv_r[l],
                                                      tables, tm)
        cqm, ckm, cvm, cqg, ckg, cvg, cy, cgates = _pre_mixer(xc, csc1, csh1, vec[l], w_in_r[l], w_uq_r[l],
                                                              w_ukv_r[l], None, tm_ctx)
        kg4, vg4, ckg4, cvg4 = kg[:, None], vg, ckg[:, None], cvg
        o_mla = _attention(qm, [km, vm, ckm, cvm], MLA_HEAD_MAP, MLA_PAIRS, tq, "attn_mla")
        o_gqa = _attention(qg, [kg4, vg4, ckg4, cvg4], GQA_HEAD_MAP, GQA_PAIRS, tq, "attn_gqa")
        z = _conv(y, cw[l], cvec[l])
        n2 = norm2_g[l][None, :]
        out_w = (wm_o[l], wc_o[l], wg_o[l], wo[l])
        x, hx, bkt, rank, cnt = _merge_route(x, g1, o_mla, z, o_gqa, gates, *out_w, sc2, sh2, n2, rw, rb,
                                             jnp.zeros((BUCKET_ROWS, LANES), F32), tm)
        moe_parts = [(x, g2, hx, bkt, rank, tm)]
        if not last:
            co_mla = _attention(cqm, [ckm, cvm], MLA_HEAD_MAP, MLA_PAIRS, tm_ctx, "attn_mla_ctx")
            co_gqa = _attention(cqg, [ckg4, cvg4], GQA_HEAD_MAP, GQA_PAIRS, tm_ctx, "attn_gqa_ctx")
            cz = _conv(cy, cw[l], cvec[l])
            xc, chx, cbkt, crank, cnt = _merge_route(xc, cg1, co_mla, cz, co_gqa, cgates, *out_w, csc2, csh2, n2,
                                                     rw, rb, cnt, tm_ctx)
            moe_parts.append((xc, cg2, chx, cbkt, crank, tm_ctx))
        outs, hs_buf = _moe(moe_parts, cnt, hs_buf, wg_e[l], wu_e[l], wd_e[l], tm_exp)
        x = outs[0]
        if not last:
            xc = outs[1]
    return x
```

```python
import functools
import math

import numpy as np
import jax
import jax.numpy as jnp
from jax import lax
from jax.experimental import pallas as pl
from jax.experimental.pallas import tpu as pltpu

GRID_W = 64
ROPE_BASE = 10000.0
EPS = 1e-6

MLA_HEADS = 8
MLA_Q_RANK = 256
MLA_KV_RANK = 128
MLA_NOPE = 64
MLA_ROPE = 32
MLA_V = 64
MLA_QK = MLA_NOPE + MLA_ROPE
CONV_CH = 512
CONV_W = 31
GQA_HEADS = 8
GQA_KV_HEADS = 2
GQA_HD = 64
N_BRANCH = 3
N_EXPERTS = 16
N_GROUPS = 4
EXPERTS_PER_GROUP = 4
D_EXPERT = 512

LANES = 128
SUBLANES = 8
HALF = LANES // 2
N_PAIRS = 6
N_BUCKETS = N_GROUPS * N_PAIRS
BUCKET_ROWS = 32
TOK_SUB = 16
OUT_SUB = 8
LOG2E = math.log2(math.e)
VMEM_LIMIT = 56 * 1024 * 1024

C_CQ = 0
C_CKV = C_CQ + MLA_Q_RANK
C_KPE = C_CKV + MLA_KV_RANK
C_GK = C_KPE + LANES
C_GV = C_GK + LANES
C_GQ = C_GV + LANES
C_CA = C_GQ + GQA_HEADS * GQA_HD
C_CG = C_CA + CONV_CH
C_GATE = C_CG + CONV_CH
D_IN_R = C_GATE

V_QN = 0
V_KVN = 256
V_QG = 384
V_KG = 512
V_GGQ = 640
V_GGK = 768

BF16 = jnp.bfloat16
F32 = jnp.float32
NT_DIMS = (((1,), (1,)), ((), ()))


def _dot(a, b):
    return jnp.dot(a, b, preferred_element_type=F32)


def _sigmoid(v):
    return 0.5 * jnp.tanh(0.5 * v) + 0.5


def _split_hi_lo(a):
    hi = a.astype(BF16)
    lo = (a - hi.astype(F32)).astype(BF16)
    return hi, lo


def _params(*sem):
    return pltpu.CompilerParams(dimension_semantics=sem, vmem_limit_bytes=VMEM_LIMIT)


def _const_spec(shape):
    nd = len(shape)
    return pl.BlockSpec(shape, lambda *_: (0,) * nd)


def _mod_kernel(c_ref, w_ref, b_ref, o_ref):
    c = c_ref[...]
    s = c * _sigmoid(c)
    sh, sl = _split_hi_lo(s)
    wh, wl = _split_hi_lo(w_ref[0])
    o_ref[0] = _dot(sh, wh) + _dot(sh, wl) + _dot(sl, wh) + b_ref[0]


def _modulation(c_all, w_mod, b_mod):
    depth, d, d6 = w_mod.shape
    m = c_all.shape[0]
    tn = 1536
    return pl.pallas_call(
        _mod_kernel,
        grid=(depth, d6 // tn),
        in_specs=[pl.BlockSpec((m, d), lambda l, j: (0, 0)),
                  pl.BlockSpec((1, d, tn), lambda l, j: (l, 0, j)),
                  pl.BlockSpec((1, 1, tn), lambda l, j: (l, 0, j))],
        out_specs=pl.BlockSpec((1, m, tn), lambda l, j: (l, 0, j)),
        out_shape=jax.ShapeDtypeStruct((depth, m, d6), F32),
        compiler_params=_params("parallel", "parallel"),
        name="modulation",
    )(c_all, w_mod, b_mod.reshape(depth, 1, d6))


def _rope(n, cos, sa, sb, quarter):
    return n * cos + pltpu.roll(n, LANES - quarter, 1) * sa + pltpu.roll(n, quarter, 1) * sb


def _pre_mixer_kernel(x_ref, sc_ref, sh_ref, vec_ref, w_ref, wuq_ref, wukv_ref, *rest, use_rope, d_model, t_sub):
    for r0 in range(0, x_ref.shape[1], t_sub):
        _pre_mixer_rows(slice(r0, r0 + t_sub), x_ref, sc_ref, sh_ref, vec_ref, w_ref, wuq_ref, wukv_ref, rest,
                        use_rope, d_model)


def _pre_mixer_rows(rows, x_ref, sc_ref, sh_ref, vec_ref, w_ref, wuq_ref, wukv_ref, rest, use_rope, d_model):
    if use_rope:
        mc_ref, msa_ref, msb_ref, gc_ref, gsa_ref, gsb_ref = rest[:6]
        rest = rest[6:]
    qm_ref, km_ref, vm_ref, qg_ref, kg_ref, vg_ref, y_ref, gate_ref = rest

    x = x_ref[0, rows, :]
    inv = lax.rsqrt(jnp.mean(x * x, axis=-1, keepdims=True) + EPS)
    h = (x * inv * vec_ref[0:1, :]) * (1.0 + sc_ref[0]) + sh_ref[0]
    hb = h.astype(BF16)

    lane = lax.broadcasted_iota(jnp.int32, (1, LANES), 1)
    lo = lane < HALF

    def proj(c0, width):
        return _dot(hb, w_ref[:, c0:c0 + width])

    def rms(v, g):
        return v * lax.rsqrt(jnp.mean(v * v, axis=-1, keepdims=True) + EPS) * g

    def seg_inv(xs, n_lo, n_hi):
        sq = xs * xs
        if n_hi is None:
            return lax.rsqrt(jnp.sum(sq, axis=-1, keepdims=True) * (1.0 / n_lo) + EPS)
        s_lo = jnp.sum(jnp.where(lo, sq, 0.0), axis=-1, keepdims=True)
        s_hi = jnp.sum(jnp.where(lo, 0.0, sq), axis=-1, keepdims=True)
        return jnp.where(lo, lax.rsqrt(s_lo * (1.0 / n_lo) + EPS), lax.rsqrt(s_hi * (1.0 / n_hi) + EPS))

    def mla_rope(n):
        return _rope(n, mc_ref[rows, :], msa_ref[rows, :], msb_ref[rows, :], MLA_ROPE // 4) if use_rope else n

    def gqa_rope(n):
        return _rope(n, gc_ref[rows, :], gsa_ref[rows, :], gsb_ref[rows, :], GQA_HD // 4) if use_rope else n

    small = proj(0, C_CA)

    def narrow(c0, width):
        return small[:, c0:c0 + width]

    cqn = rms(narrow(C_CQ, MLA_Q_RANK), vec_ref[1:2, V_QN:V_QN + MLA_Q_RANK])
    q = _dot(cqn.astype(BF16), wuq_ref[...])
    qgain = vec_ref[1:2, V_QG:V_QG + LANES]
    for hh in range(MLA_HEADS):
        xs = q[:, hh * LANES:(hh + 1) * LANES]
        n = mla_rope(xs * seg_inv(xs, MLA_NOPE, MLA_ROPE) * qgain)
        qm_ref[0, hh, rows, :] = (n * (MLA_QK ** -0.5 * LOG2E)).astype(BF16)

    kgain = vec_ref[1:2, V_KG:V_KG + LANES]
    kpe = narrow(C_KPE, LANES)
    kpe = mla_rope(kpe * seg_inv(kpe, MLA_ROPE, None) * kgain)
    ckvn = rms(narrow(C_CKV, MLA_KV_RANK), vec_ref[1:2, V_KVN:V_KVN + MLA_KV_RANK])
    kv = _dot(ckvn.astype(BF16), wukv_ref[...])
    for hh in range(MLA_HEADS):
        xs = kv[:, hh * LANES:(hh + 1) * LANES]
        km_ref[0, hh, rows, :] = (xs * seg_inv(xs, MLA_NOPE, None) * kgain + kpe).astype(BF16)
    ones_slab = jnp.ones((x.shape[0], LANES), BF16)
    v0 = MLA_HEADS * LANES
    for p in range(MLA_HEADS // 2):
        pair = kv[:, v0 + p * LANES:v0 + (p + 1) * LANES].astype(BF16)
        vm_ref[0, 2 * p, rows, :] = jnp.where(lo, pair, ones_slab)
        vm_ref[0, 2 * p + 1, rows, :] = jnp.where(lo, ones_slab, pair)

    gk = narrow(C_GK, LANES)
    ggk = vec_ref[1:2, V_GGK:V_GGK + LANES]
    kg_ref[0, rows, :] = gqa_rope(gk * seg_inv(gk, GQA_HD, GQA_HD) * ggk).astype(BF16)
    gv = narrow(C_GV, LANES).astype(BF16)
    vg_ref[0, 0, rows, :] = jnp.where(lo, gv, ones_slab)
    vg_ref[0, 1, rows, :] = jnp.where(lo, ones_slab, gv)
    ggq = vec_ref[1:2, V_GGQ:V_GGQ + LANES]
    for p in range(GQA_HEADS // 2):
        xs = narrow(C_GQ + p * LANES, LANES)
        n = gqa_rope(xs * seg_inv(xs, GQA_HD, GQA_HD) * ggq)
        qg_ref[0, p, rows, :] = (n * (GQA_HD ** -0.5 * LOG2E)).astype(BF16)

    y_ref[0, rows, :] = proj(C_CA, CONV_CH) * _sigmoid(proj(C_CG, CONV_CH))
    for j in range(N_BRANCH):
        gate_ref[0, rows, j * d_model:(j + 1) * d_model] = _sigmoid(
            proj(C_GATE + j * d_model, d_model)).astype(BF16)


PRE_SUB = 512


def _pre_mixer(x, sc, sh, vec, w_in_r, w_uq_r, w_ukv_r, tables, tm):
    b, n, d = x.shape
    use_rope = tables is not None
    nt = n // tm
    bmap = lambda i, j: (j, 0, 0)
    in_specs = [pl.BlockSpec((1, tm, d), lambda i, j: (j, i, 0)),
                pl.BlockSpec((1, 1, d), bmap),
                pl.BlockSpec((1, 1, d), bmap),
                _const_spec(vec.shape), _const_spec(w_in_r.shape), _const_spec(w_uq_r.shape),
                _const_spec(w_ukv_r.shape)]
    args = [x, sc, sh, vec, w_in_r, w_uq_r, w_ukv_r]
    if use_rope:
        in_specs += [pl.BlockSpec((tm, LANES), lambda i, j: (i, 0))] * 6
        args += list(tables)
    hspec = lambda nh, w=LANES: pl.BlockSpec((1, nh, tm, w), lambda i, j: (j, 0, i, 0))
    tspec = lambda w: pl.BlockSpec((1, tm, w), lambda i, j: (j, i, 0))
    out_specs = [hspec(MLA_HEADS), hspec(MLA_HEADS), hspec(MLA_HEADS), hspec(GQA_HEADS // 2),
                 tspec(LANES), hspec(GQA_KV_HEADS), tspec(CONV_CH), tspec(N_BRANCH * d)]
    sds = jax.ShapeDtypeStruct
    out_shape = [sds((b, MLA_HEADS, n, LANES), BF16), sds((b, MLA_HEADS, n, LANES), BF16),
                 sds((b, MLA_HEADS, n, LANES), BF16), sds((b, GQA_HEADS // 2, n, LANES), BF16),
                 sds((b, n, LANES), BF16), sds((b, GQA_KV_HEADS, n, LANES), BF16),
                 sds((b, n, CONV_CH), F32), sds((b, n, N_BRANCH * d), BF16)]
    return pl.pallas_call(
        functools.partial(_pre_mixer_kernel, use_rope=use_rope, d_model=d, t_sub=min(tm, PRE_SUB)),
        grid=(nt, b), in_specs=in_specs, out_specs=out_specs, out_shape=out_shape,
        compiler_params=_params("parallel", "parallel"),
        name="pre_mixer_rope" if use_rope else "pre_mixer",
    )(*args)


def _attn_kernel(q_ref, *refs, n_seg, head_map, pairs, t_sub, t_piece):
    kv_refs = refs[:2 * n_seg]
    o_ref = refs[2 * n_seg]
    lane = lax.broadcasted_iota(jnp.int32, (1, LANES), 1)
    lo = lane < HALF

    def one_head(hh, rows, out_half):
        qidx, q_half, kidx, vidx = head_map[hh]
        sum_lane = HALF if out_half == 0 else 0
        q = q_ref[0, qidx, rows, :]
        if q_half is not None:
            q = jnp.where(lo if q_half == 0 else jnp.logical_not(lo), q, jnp.zeros_like(q))
        ss_all = [lax.dot_general(q, kv_refs[2 * s][0, kidx], NT_DIMS, preferred_element_type=F32)
                  for s in range(n_seg)]
        outs = []
        for p0 in range(0, ss_all[0].shape[0], t_piece):
            ss = [s[p0:p0 + t_piece] for s in ss_all]
            m = functools.reduce(jnp.maximum, [jnp.max(s, axis=-1, keepdims=True) for s in ss])
            ol = functools.reduce(jnp.add, [_dot(jnp.exp2(ss[s] - m).astype(BF16), kv_refs[2 * s + 1][0, vidx])
                                            for s in range(n_seg)])
            outs.append(ol / ol[:, sum_lane:sum_lane + 1])
        return outs[0] if len(outs) == 1 else jnp.concatenate(outs, axis=0)

    for r0 in range(0, q_ref.shape[2], t_sub):
        rows = slice(r0, r0 + t_sub)
        for p, (h_lo, h_hi) in enumerate(pairs):
            o_ref[0, rows, p * LANES:(p + 1) * LANES] = jnp.where(
                lo, one_head(h_lo, rows, 0), one_head(h_hi, rows, 1)).astype(BF16)


MLA_HEAD_MAP = tuple((h, None, h, h) for h in range(MLA_HEADS))
MLA_PAIRS = tuple((2 * p, 2 * p + 1) for p in range(MLA_HEADS // 2))
GQA_HEAD_MAP = tuple((h % (GQA_HEADS // 2), h // (GQA_HEADS // 2), 0, h // (GQA_HEADS // 2))
                     for h in range(GQA_HEADS))
GQA_PAIRS = tuple((a, a + GQA_HEADS // 2) for a in range(GQA_HEADS // 2))


ATTN_SUB = 512
ATTN_PIECE = 256


def _attention(q, kvs, head_map, pairs, tq, name):
    b, nh, n, _ = q.shape
    in_specs = [pl.BlockSpec((1, nh, tq, LANES), lambda i, j: (i, 0, j, 0))]
    args = [q]
    for a in kvs:
        in_specs.append(pl.BlockSpec((1,) + a.shape[1:], lambda i, j: (i, 0, 0, 0)))
        args.append(a)
    width = len(pairs) * LANES
    return pl.pallas_call(
        functools.partial(_attn_kernel, n_seg=len(kvs) // 2, head_map=head_map, pairs=pairs,
                          t_sub=min(tq, ATTN_SUB), t_piece=min(tq, ATTN_PIECE)),
        grid=(b, n // tq), in_specs=in_specs,
        out_specs=pl.BlockSpec((1, tq, width), lambda i, j: (i, j, 0)),
        out_shape=jax.ShapeDtypeStruct((b, n, width), BF16),
        compiler_params=_params("parallel", "parallel"),
        name=name,
    )(*args)


CONV_PAD = 16
CONV_TAIL = 24
CONV_CHUNK = 256


def _conv_kernel(y_ref, cw_ref, vec_ref, z_ref, buf_ref, ph_ref, *, n):
    ch = y_ref.shape[-1]
    buf_ref[0:CONV_PAD, :] = jnp.zeros((CONV_PAD, ch), F32)
    buf_ref[CONV_PAD + n:CONV_PAD + n + CONV_TAIL, :] = jnp.zeros((CONV_TAIL, ch), F32)
    buf_ref[CONV_PAD:CONV_PAD + n, :] = y_ref[0]
    first = CONV_PAD - CONV_W // 2
    rows = ph_ref.shape[1] - CONV_TAIL
    win_rows = rows + CONV_W + 1

    def chunk(c, carry):
        r0 = pl.multiple_of(c * rows, rows)
        win = buf_ref[pl.ds(r0, win_rows), :]
        for s in range(SUBLANES):
            ph_ref[s] = win[s:s + ph_ref.shape[1], :]
        acc = jnp.zeros((rows, ch), F32) + vec_ref[0:1, :]
        for k in range(CONV_W):
            a, s = divmod(first + k, SUBLANES)
            acc = acc + ph_ref[s, a * SUBLANES:a * SUBLANES + rows, :] * cw_ref[k:k + 1, :]
        mu = jnp.mean(acc, axis=-1, keepdims=True)
        cen = acc - mu
        var = jnp.mean(cen * cen, axis=-1, keepdims=True)
        yn = cen * lax.rsqrt(var + EPS) * vec_ref[1:2, :] + vec_ref[2:3, :]
        z_ref[0, pl.ds(r0, rows), :] = (yn * _sigmoid(yn)).astype(BF16)
        return carry

    lax.fori_loop(0, n // rows, chunk, 0)


def _conv(y, cw, cvec):
    b, n, ch = y.shape
    chunk = min(CONV_CHUNK, n)
    assert n % chunk == 0 and chunk % SUBLANES == 0
    return pl.pallas_call(
        functools.partial(_conv_kernel, n=n),
        grid=(b,),
        in_specs=[pl.BlockSpec((1, n, ch), lambda i: (i, 0, 0)), _const_spec(cw.shape), _const_spec(cvec.shape)],
        out_specs=pl.BlockSpec((1, n, ch), lambda i: (i, 0, 0)),
        out_shape=jax.ShapeDtypeStruct((b, n, ch), BF16),
        scratch_shapes=[pltpu.VMEM((CONV_PAD + n + CONV_TAIL, ch), F32),
                        pltpu.VMEM((SUBLANES, chunk + CONV_TAIL, ch), F32)],
        compiler_params=_params("parallel"),
        name="conv",
    )(y, cw, cvec)


def _merge_route_kernel(x_ref, g_ref, om_ref, oc_ref, og_ref, gate_ref, wm_ref, wc_ref, wg_ref, wo_ref,
                        sc_ref, sh_ref, n2_ref, rw_ref, rb_ref, cin_ref,
                        o_ref, hx_ref, bkt_ref, rank_ref, cnt_ref, *, d_model):
    d = d_model
    merged = (gate_ref[0, :, 0:d].astype(F32) * _dot(om_ref[0], wm_ref[...])
              + gate_ref[0, :, d:2 * d].astype(F32) * _dot(oc_ref[0], wc_ref[...])
              + gate_ref[0, :, 2 * d:3 * d].astype(F32) * _dot(og_ref[0], wg_ref[...]))
    x_new = x_ref[0] + g_ref[0] * _dot(merged.astype(BF16), wo_ref[...])
    o_ref[0] = x_new
    _route(x_new, sc_ref, sh_ref, n2_ref, rw_ref, rb_ref, cin_ref, hx_ref, bkt_ref, rank_ref, cnt_ref, d)


def _merge_route(x, g1, o_mla, z, o_gqa, gates, wm, wc, wg, wo, sc2, sh2, n2, rw, rb, cin, tm):
    b, n, d = x.shape
    nt = n // tm
    sds = jax.ShapeDtypeStruct
    tok = lambda w: pl.BlockSpec((1, tm, w), lambda i, j: (i, j, 0))
    per_b = pl.BlockSpec((1, 1, d), lambda i, j: (i, 0, 0))
    ispec = pl.BlockSpec((1, 1, 1, tm), lambda i, j: (i, j, 0, 0))
    return pl.pallas_call(
        functools.partial(_merge_route_kernel, d_model=d),
        grid=(b, nt),
        in_specs=[tok(d), per_b, tok(o_mla.shape[-1]), tok(z.shape[-1]), tok(o_gqa.shape[-1]), tok(N_BRANCH * d),
                  _const_spec(wm.shape), _const_spec(wc.shape), _const_spec(wg.shape), _const_spec(wo.shape),
                  per_b, per_b, _const_spec(n2.shape), _const_spec(rw.shape), _const_spec(rb.shape),
                  _const_spec(cin.shape)],
        out_specs=[tok(d), pl.BlockSpec((tm * TOK_SUB, LANES), lambda i, j: (i * nt + j, 0)), ispec, ispec,
                   _const_spec((BUCKET_ROWS, LANES))],
        out_shape=[sds((b, n, d), F32), sds((b * n * TOK_SUB, LANES), F32), sds((b, nt, 1, tm), jnp.int32),
                   sds((b, nt, 1, tm), jnp.int32), sds((BUCKET_ROWS, LANES), F32)],
        compiler_params=_params("arbitrary", "arbitrary"),
        name="merge_route",
    )(x, g1, o_mla, z, o_gqa, gates, wm, wc, wg, wo, sc2, sh2, n2, rw, rb, cin)


def _first_argmax4(v):
    m = functools.reduce(jnp.maximum, v)
    idx = jnp.where(v[0] == m, 0, jnp.where(v[1] == m, 1, jnp.where(v[2] == m, 2, 3)))
    return m, idx


def _pick4(idx, v):
    return jnp.where(idx == 0, v[0], jnp.where(idx == 1, v[1], jnp.where(idx == 2, v[2], v[3])))


def _route(x, sc_ref, sh_ref, g_ref, rw_ref, rb_ref, cin_ref, hx_ref, bkt_ref, rank_ref, cnt_ref, d_model):
    tm = x.shape[0]
    first = (pl.program_id(0) == 0) & (pl.program_id(1) == 0)

    @pl.when(first)
    def _():
        cnt_ref[...] = cin_ref[...]

    inv = lax.rsqrt(jnp.mean(x * x, axis=-1, keepdims=True) + EPS)
    h = (x * inv * g_ref[...]) * (1.0 + sc_ref[0]) + sh_ref[0]
    for j in range(d_model // LANES):
        hx_ref[pl.ds(j, tm, stride=TOK_SUB), :] = h[:, j * LANES:(j + 1) * LANES]

    hh, hl = _split_hi_lo(h)
    rwh, rwl = rw_ref[0], rw_ref[1]
    logits = (lax.dot_general(rwh, hh, NT_DIMS, preferred_element_type=F32)
              + lax.dot_general(rwh, hl, NT_DIMS, preferred_element_type=F32)
              + lax.dot_general(rwl, hh, NT_DIMS, preferred_element_type=F32))
    s_all = _sigmoid(logits)
    sel_all = s_all + rb_ref[...]
    s = [s_all[e:e + 1, :] for e in range(N_EXPERTS)]
    sel = [sel_all[e:e + 1, :] for e in range(N_EXPERTS)]

    gscore = []
    for g in range(N_GROUPS):
        a, b, c, d = sel[4 * g:4 * g + 4]
        gscore.append(functools.reduce(jnp.maximum, [a + b, a + c, a + d, b + c, b + d, c + d]))
    _, gi = _first_argmax4(gscore)
    v = [_pick4(gi, [sel[4 * g + j] for g in range(N_GROUPS)]) for j in range(EXPERTS_PER_GROUP)]
    u = [_pick4(gi, [s[4 * g + j] for g in range(N_GROUPS)]) for j in range(EXPERTS_PER_GROUP)]
    _, i1 = _first_argmax4(v)
    _, i2 = _first_argmax4([jnp.where(i1 == j, -jnp.inf, v[j]) for j in range(EXPERTS_PER_GROUP)])
    e_lo = jnp.minimum(i1, i2)
    e_hi = jnp.maximum(i1, i2)
    pair = jnp.where(e_lo == 0, e_hi - 1, jnp.where(e_lo == 1, e_hi + 1, N_PAIRS - 1))
    bucket = gi * N_PAIRS + pair
    u_lo = _pick4(e_lo, u)
    u_hi = _pick4(e_hi, u)
    den = u_lo + u_hi
    wts = jnp.concatenate([u_lo / den, u_hi / den, jnp.zeros((LANES - 2, tm), F32)], axis=0)
    n_feat = d_model // LANES
    hx_ref[pl.ds(n_feat, tm, stride=TOK_SUB), :] = wts.T
    for j in range(n_feat + 1, TOK_SUB):
        hx_ref[pl.ds(j, tm, stride=TOK_SUB), :] = jnp.zeros((tm, LANES), F32)

    onehot = lax.broadcasted_iota(jnp.int32, (BUCKET_ROWS, tm), 0) == bucket
    oh = jnp.where(onehot, 1.0, 0.0)
    before = lax.broadcasted_iota(jnp.int32, (tm, tm), 0) < lax.broadcasted_iota(jnp.int32, (tm, tm), 1)
    cum = _dot(oh.astype(BF16), jnp.where(before, 1.0, 0.0).astype(BF16))
    rank = jnp.sum(oh * (cum + cnt_ref[:, 0:1]), axis=0, keepdims=True)
    bkt_ref[0, 0] = bucket
    rank_ref[0, 0] = rank.astype(jnp.int32)
    cnt_ref[...] = cnt_ref[...] + jnp.sum(oh, axis=1, keepdims=True)


ROW_UNROLL = 4
N_DMA_PRIORITIES = 2


def _token_rows(t, sub):
    return pl.ds(pl.multiple_of(t * sub, sub), sub)


def _issue_rows(tm, row_copy, pos_ref):
    def issue(i, c):
        for q in range(N_DMA_PRIORITIES):
            r = i * N_DMA_PRIORITIES + q
            row_copy(r, pos_ref[0, 0, r]).start(priority=q)
        return c

    lax.fori_loop(0, tm // N_DMA_PRIORITIES, issue, 0, unroll=ROW_UNROLL)


def _drain_rows(tm, row_copy):
    def drain(r, c):
        row_copy(0, 0).wait()
        return c

    lax.fori_loop(0, tm, drain, 0, unroll=SUBLANES)


def _scatter_kernel(pos_ref, src_ref, hs_in_ref, hs_ref, sem):
    del hs_in_ref
    tm = src_ref.shape[0] // TOK_SUB

    def row_copy(r, p):
        return pltpu.make_async_copy(src_ref.at[_token_rows(r, TOK_SUB)], hs_ref.at[_token_rows(p, TOK_SUB)], sem)

    _issue_rows(tm, row_copy, pos_ref)
    _drain_rows(tm, row_copy)


def _scatter(pos3, src, hs, tm):
    ispec = pl.BlockSpec((1, 1, tm), lambda i: (i, 0, 0), memory_space=pltpu.SMEM)
    return pl.pallas_call(
        _scatter_kernel,
        grid=(src.shape[0] // (tm * TOK_SUB),),
        in_specs=[ispec, pl.BlockSpec((tm * TOK_SUB, LANES), lambda i: (i, 0)), pl.BlockSpec(memory_space=pl.ANY)],
        out_specs=pl.BlockSpec(memory_space=pl.ANY),
        out_shape=jax.ShapeDtypeStruct(hs.shape, hs.dtype),
        scratch_shapes=[pltpu.SemaphoreType.DMA(())],
        input_output_aliases={2: 0},
        compiler_params=_params("arbitrary"),
        name="scatter_rows",
    )(pos3, src, hs)


def _gather_res_kernel(cur_ref, nxt_ref, x_ref, g_ref, ys_ref, o_ref, buf_ref, sems):
    step = pl.program_id(0)
    tm = x_ref.shape[1]

    def row_copy_on(slot):
        def row_copy(r, p):
            return pltpu.make_async_copy(ys_ref.at[_token_rows(p, OUT_SUB)],
                                         buf_ref.at[slot, _token_rows(r, OUT_SUB)], sems.at[slot])
        return row_copy

    slot = step % 2

    @pl.when(step == 0)
    def _():
        _issue_rows(tm, row_copy_on(slot), cur_ref)

    @pl.when(step + 1 < pl.num_programs(0))
    def _():
        _issue_rows(tm, row_copy_on(1 - slot), nxt_ref)

    _drain_rows(tm, row_copy_on(slot))
    moe = jnp.concatenate([buf_ref[slot, pl.ds(j, tm, stride=OUT_SUB), :] for j in range(OUT_SUB)], axis=-1)
    o_ref[0] = x_ref[0] + g_ref[0] * moe


def _gather_res(pos3, x, g2, ys, tm):
    b, n, d = x.shape
    nt = n // tm
    steps = b * nt
    cur = pl.BlockSpec((1, 1, tm), lambda i: (i, 0, 0), memory_space=pltpu.SMEM)
    nxt = pl.BlockSpec((1, 1, tm), lambda i: (jnp.minimum(i + 1, steps - 1), 0, 0), memory_space=pltpu.SMEM)
    tok = pl.BlockSpec((1, tm, d), lambda i: (i // nt, i % nt, 0))
    return pl.pallas_call(
        _gather_res_kernel,
        grid=(steps,),
        in_specs=[cur, nxt, tok, pl.BlockSpec((1, 1, d), lambda i: (i // nt, 0, 0)), pl.BlockSpec(memory_space=pl.ANY)],
        out_specs=tok,
        out_shape=jax.ShapeDtypeStruct(x.shape, F32),
        scratch_shapes=[pltpu.VMEM((2, tm * OUT_SUB, LANES), F32), pltpu.SemaphoreType.DMA((2,))],
        compiler_params=_params("arbitrary"),
        name="gather_residual",
    )(pos3, pos3, x, g2, ys)


def _experts_kernel(e1_ref, e2_ref, blk_ref, valid_ref, hs_ref, wg1_ref, wu1_ref, wd1_ref, wg2_ref, wu2_ref,
                    wd2_ref, o_ref, *, d_model):
    valid = valid_ref[pl.program_id(0)] > 0

    @pl.when(jnp.logical_not(valid))
    def _():
        o_ref[...] = jnp.zeros(o_ref.shape, F32)

    @pl.when(valid)
    def _():
        tm = hs_ref.shape[0] // TOK_SUB
        n_feat = d_model // LANES

        def slab(j):
            return hs_ref[pl.ds(j, tm, stride=TOK_SUB), :]

        xb = jnp.concatenate([slab(j).astype(BF16) for j in range(n_feat)], axis=-1)
        wts = slab(n_feat)

        def half(wg_ref, wu_ref, w):
            gte = _dot(xb, wg_ref[0])
            return (gte * _sigmoid(gte) * _dot(xb, wu_ref[0]) * w).astype(BF16)

        h1 = half(wg1_ref, wu1_ref, wts[:, 0:1])
        h2 = half(wg2_ref, wu2_ref, wts[:, 1:2])
        out = _dot(h1, wd1_ref[0]) + _dot(h2, wd2_ref[0])
        for j in range(n_feat):
            o_ref[pl.ds(j, tm, stride=OUT_SUB), :] = out[:, j * LANES:(j + 1) * LANES]


def _experts(e1, e2, blk, valid, hs, wg, wu, wd, tm):
    tp = hs.shape[0] // TOK_SUB
    d, f = wg.shape[-2:]
    assert d == OUT_SUB * LANES
    up1 = pl.BlockSpec((1, d, f), lambda i, e1, e2, blk, valid: (e1[i], 0, 0))
    up2 = pl.BlockSpec((1, d, f), lambda i, e1, e2, blk, valid: (e2[i], 0, 0))
    dn1 = pl.BlockSpec((1, f, d), lambda i, e1, e2, blk, valid: (e1[i], 0, 0))
    dn2 = pl.BlockSpec((1, f, d), lambda i, e1, e2, blk, valid: (e2[i], 0, 0))
    grid_spec = pltpu.PrefetchScalarGridSpec(
        num_scalar_prefetch=4, grid=(tp // tm,),
        in_specs=[pl.BlockSpec((tm * TOK_SUB, LANES), lambda i, e1, e2, blk, valid: (blk[i], 0)),
                  up1, up1, dn1, up2, up2, dn2],
        out_specs=pl.BlockSpec((tm * OUT_SUB, LANES), lambda i, e1, e2, blk, valid: (i, 0)))
    return pl.pallas_call(
        functools.partial(_experts_kernel, d_model=d),
        grid_spec=grid_spec,
        out_shape=jax.ShapeDtypeStruct((tp * OUT_SUB, LANES), F32),
        compiler_params=_params("arbitrary"),
        name="moe_experts",
    )(e1, e2, blk, valid, hs, wg, wu, wd, wg, wu, wd)


PAIR_LO = np.array([0, 0, 0, 1, 1, 2], np.int32)
PAIR_HI = np.array([1, 2, 3, 2, 3, 3], np.int32)


def _moe_plan(counts, n_tiles, tm):
    c = counts[:N_BUCKETS]
    nt = (c + tm - 1) // tm
    tile_end = jnp.cumsum(nt)
    row_off = (tile_end - nt) * tm
    n_valid = tile_end[-1]
    i = jnp.arange(n_tiles, dtype=jnp.int32)
    blk = jnp.minimum(i, n_valid - 1)
    bkt = jnp.minimum(jnp.sum(blk[:, None] >= tile_end[None, :], axis=1), N_BUCKETS - 1).astype(jnp.int32)
    grp = bkt // N_PAIRS
    e1 = grp * EXPERTS_PER_GROUP + jnp.asarray(PAIR_LO)[bkt % N_PAIRS]
    e2 = grp * EXPERTS_PER_GROUP + jnp.asarray(PAIR_HI)[bkt % N_PAIRS]
    return row_off, e1, e2, blk.astype(jnp.int32), (i < n_valid).astype(jnp.int32)


def _sorted_rows_buffer(n_tokens, tm_exp):
    n_tiles = -(-n_tokens // tm_exp) + N_BUCKETS
    return jnp.zeros((n_tiles * tm_exp * TOK_SUB, LANES), F32)


def _moe(parts, cnt, hs, wg, wu, wd, tm_exp):
    n_tiles = hs.shape[0] // (TOK_SUB * tm_exp)
    row_off, e1, e2, blk, valid = _moe_plan(cnt[:, 0].astype(jnp.int32), n_tiles, tm_exp)
    addr = []
    for x, g2, hx, bkt, rank, tm in parts:
        start = sum(jnp.where(bkt == k, row_off[k], 0) for k in range(N_BUCKETS))
        pos = (start + rank).reshape(-1, 1, tm)
        addr.append(pos)
        hs = _scatter(pos, hx, hs, tm)
    ys = _experts(e1, e2, blk, valid, hs, wg, wu, wd, tm_exp)
    outs = [_gather_res(pos, x, g2, ys, tm) for (x, g2, hx, bkt, rank, tm), pos in zip(parts, addr)]
    return outs, hs


def _relayout_w_in(w_in, d_model):
    sizes = (MLA_Q_RANK, MLA_KV_RANK, MLA_ROPE, 2 * CONV_CH, GQA_HEADS * GQA_HD, GQA_KV_HEADS * GQA_HD,
             GQA_KV_HEADS * GQA_HD, N_BRANCH * d_model)
    offs = np.cumsum((0,) + sizes)
    cq, ckv, kpe, conv, gq, gk, gv, gates = [w_in[..., offs[i]:offs[i + 1]] for i in range(len(sizes))]
    lead = w_in.shape[:-1]
    z = lambda n: jnp.zeros(lead + (n,), w_in.dtype)
    head = lambda hh: gq[..., hh * GQA_HD:(hh + 1) * GQA_HD]
    gq_slabs = [head(hh) for p in range(GQA_HEADS // 2) for hh in (p, p + GQA_HEADS // 2)]
    out = jnp.concatenate([cq, ckv, z(HALF), kpe, z(HALF - MLA_ROPE), gk, gv] + gq_slabs + [conv, gates], axis=-1)
    return out.astype(BF16)


def _relayout_uq(w_uq):
    l, r, _ = w_uq.shape
    w = w_uq.reshape(l, r, MLA_HEADS, MLA_QK)
    w = jnp.pad(w, ((0, 0), (0, 0), (0, 0), (0, LANES - MLA_QK)))
    return w.reshape(l, r, MLA_HEADS * LANES).astype(BF16)


def _relayout_ukv(w_ukv):
    l, r, _ = w_ukv.shape
    w = w_ukv.reshape(l, r, MLA_HEADS, MLA_NOPE + MLA_V)
    k = jnp.pad(w[..., :MLA_NOPE], ((0, 0), (0, 0), (0, 0), (0, LANES - MLA_NOPE)))
    v = w[..., MLA_NOPE:]
    return jnp.concatenate([k.reshape(l, r, MLA_HEADS * LANES), v.reshape(l, r, MLA_HEADS * MLA_V)],
                           axis=-1).astype(BF16)


def _rope_tables(n, r):
    rows = n // GRID_W
    row = jnp.repeat(jnp.arange(rows, dtype=jnp.int32), GRID_W).astype(F32)
    col = jnp.tile(jnp.arange(GRID_W, dtype=jnp.int32), rows).astype(F32)
    dim = r // 2
    inv = ROPE_BASE ** (-jnp.arange(0, dim, 2, dtype=F32) / dim)
    ar, ac = row[:, None] * inv[None, :], col[:, None] * inv[None, :]
    zero = jnp.zeros_like(ar)
    cos = jnp.concatenate([jnp.cos(ar), jnp.cos(ar), jnp.cos(ac), jnp.cos(ac)], axis=-1)
    sa = jnp.concatenate([-jnp.sin(ar), zero, -jnp.sin(ac), zero], axis=-1)
    sb = jnp.concatenate([zero, jnp.sin(ar), zero, jnp.sin(ac)], axis=-1)
    return cos, sa, sb


def _all_tables(n):
    mc, msa, msb = _rope_tables(n, MLA_ROPE)
    one = jnp.ones((n, MLA_NOPE), F32)
    zero = jnp.zeros((n, MLA_NOPE), F32)
    pad1 = jnp.ones((n, LANES - MLA_QK), F32)
    pad0 = jnp.zeros((n, LANES - MLA_QK), F32)
    mla = (jnp.concatenate([one, mc, pad1], -1), jnp.concatenate([zero, msa, pad0], -1),
           jnp.concatenate([zero, msb, pad0], -1))
    gqa = tuple(jnp.concatenate([t, t], -1) for t in _rope_tables(n, GQA_HD))
    return mla + gqa


def kernel(x, c, ctx, c_ctx, w_mod, b_mod, norm1_g, norm2_g, w_in, mla_q_norm, mla_kv_norm, mla_w_uq, mla_w_ukv,
           mla_q_gain, mla_k_gain, mla_w_o, conv_w, conv_b, conv_ln_g, conv_ln_b, conv_w_o, gqa_q_gain, gqa_k_gain,
           gqa_w_o, w_out, router_w, router_bias, moe_w_gate, moe_w_up, moe_w_down):
    b, n, d = x.shape
    n_ctx = ctx.shape[1]
    depth = w_mod.shape[0]
    assert n % GRID_W == 0 and n % 512 == 0 and n_ctx % LANES == 0 and d % LANES == 0

    tm = 512
    tm_ctx = min(n_ctx, 256)
    tq = 512
    tm_exp = 512

    m_rows = -(-(b + 1) // 8) * 8
    c_all = jnp.concatenate([c, c_ctx[None, :], jnp.zeros((m_rows - b - 1, d), F32)], axis=0)
    mod = _modulation(c_all, w_mod, b_mod)

    w_in_r = _relayout_w_in(w_in, d)
    w_uq_r = _relayout_uq(mla_w_uq)
    w_ukv_r = _relayout_ukv(mla_w_ukv)
    zl = lambda k: jnp.zeros((depth, k), F32)
    qg128 = jnp.concatenate([mla_q_gain, zl(LANES - MLA_QK)], -1)
    kg128 = jnp.concatenate([mla_k_gain, zl(LANES - MLA_QK)], -1)
    row1 = jnp.concatenate([mla_q_norm, mla_kv_norm, qg128, kg128, gqa_q_gain, gqa_q_gain, gqa_k_gain, gqa_k_gain,
                            zl(d - V_GGK - LANES)], -1)
    vec = jnp.stack([norm1_g, row1], axis=1)
    gqa_perm = np.concatenate([np.arange(GQA_HD) + GQA_HD * hh for pair in GQA_PAIRS for hh in pair])
    wm_o = mla_w_o.astype(BF16)
    wg_o = gqa_w_o[:, gqa_perm, :].astype(BF16)
    wc_o = conv_w_o.astype(BF16)
    wo = w_out.astype(BF16)
    cw = jnp.pad(conv_w, ((0, 0), (0, 32 - CONV_W), (0, 0)))
    cvec = jnp.pad(jnp.stack([conv_b, conv_ln_g, conv_ln_b], axis=1), ((0, 0), (0, 5), (0, 0)))
    rwt = router_w.T
    rw_hi = rwt.astype(BF16)
    rw = jnp.stack([rw_hi, (rwt - rw_hi.astype(F32)).astype(BF16)])
    rb = router_bias.reshape(N_EXPERTS, 1)
    wg_e, wu_e, wd_e = moe_w_gate.astype(BF16), moe_w_up.astype(BF16), moe_w_down.astype(BF16)
    tables = _all_tables(n)

    xc = ctx
    assert d == OUT_SUB * LANES and OUT_SUB < TOK_SUB
    hs_buf = _sorted_rows_buffer(b * (n + n_ctx), tm_exp)
    for l in range(depth):
        last = l == depth - 1
        sh1, sc1, g1, sh2, sc2, g2 = [t[:, None, :] for t in jnp.split(mod[l, :b], 6, axis=-1)]
        csh1, csc1, cg1, csh2, csc2, cg2 = [jnp.broadcast_to(t[None, :, :], (b, 1, d))
                                            for t in jnp.split(mod[l, b:b + 1], 6, axis=-1)]
        qm, km, vm, qg, kg, vg, y, gates = _pre_mixer(x, sc1, sh1, vec[l], w_in_r[l], w_uq_r[l], w_ukv_r[l],
                                                      tables, tm)
        cqm, ckm, cvm, cqg, ckg, cvg, cy, cgates = _pre_mixer(xc, csc1, csh1, vec[l], w_in_r[l], w_uq_r[l],
                                                              w_ukv_r[l], None, tm_ctx)
        kg4, vg4, ckg4, cvg4 = kg[:, None], vg, ckg[:, None], cvg
        o_mla = _attention(qm, [km, vm, ckm, cvm], MLA_HEAD_MAP, MLA_PAIRS, tq, "attn_mla")
        o_gqa = _attention(qg, [kg4, vg4, ckg4, cvg4], GQA_HEAD_MAP, GQA_PAIRS, tq, "attn_gqa")
        z = _conv(y, cw[l], cvec[l])
        n2 = norm2_g[l][None, :]
        out_w = (wm_o[l], wc_o[l], wg_o[l], wo[l])
        x, hx, bkt, rank, cnt = _merge_route(x, g1, o_mla, z, o_gqa, gates, *out_w, sc2, sh2, n2, rw, rb,
                                             jnp.zeros((BUCKET_ROWS, LANES), F32), tm)
        moe_parts = [(x, g2, hx, bkt, rank, tm)]
        if not last:
            co_mla = _attention(cqm, [ckm, cvm], MLA_HEAD_MAP, MLA_PAIRS, tm_ctx, "attn_mla_ctx")
            co_gqa = _attention(cqg, [ckg4, cvg4], GQA_HEAD_MAP, GQA_PAIRS, tm_ctx, "attn_gqa_ctx")
            cz = _conv(cy, cw[l], cvec[l])
            xc, chx, cbkt, crank, cnt = _merge_route(xc, cg1, co_mla, cz, co_gqa, cgates, *out_w, csc2, csh2, n2,
                                                     rw, rb, cnt, tm_ctx)
            moe_parts.append((xc, cg2, chx, cbkt, crank, tm_ctx))
        outs, hs_buf = _moe(moe_parts, cnt, hs_buf, wg_e[l], wu_e[l], wd_e[l], tm_exp)
        x = outs[0]
        if not last:
            xc = outs[1]
    return x
```

```python
import functools
import math

import numpy as np
import jax
import jax.numpy as jnp
from jax import lax
from jax.experimental import pallas as pl
from jax.experimental.pallas import tpu as pltpu

GRID_W = 64
ROPE_BASE = 10000.0
EPS = 1e-6

MLA_HEADS = 8
MLA_Q_RANK = 256
MLA_KV_RANK = 128
MLA_NOPE = 64
MLA_ROPE = 32
MLA_V = 64
MLA_QK = MLA_NOPE + MLA_ROPE
CONV_CH = 512
CONV_W = 31
GQA_HEADS = 8
GQA_KV_HEADS = 2
GQA_HD = 64
N_BRANCH = 3
N_EXPERTS = 16
N_GROUPS = 4
EXPERTS_PER_GROUP = 4
D_EXPERT = 512

LANES = 128
SUBLANES = 8
HALF = LANES // 2
N_PAIRS = 6
N_BUCKETS = N_GROUPS * N_PAIRS
BUCKET_ROWS = 32
OUT_SUB = 8
LOG2E = math.log2(math.e)
VMEM_LIMIT = 56 * 1024 * 1024

C_CQ = 0
C_CKV = C_CQ + MLA_Q_RANK
C_KPE = C_CKV + MLA_KV_RANK
C_GK = C_KPE + LANES
C_GV = C_GK + LANES
C_GQ = C_GV + LANES
C_CA = C_GQ + GQA_HEADS * GQA_HD
C_CG = C_CA + CONV_CH
C_GATE = C_CG + CONV_CH
D_IN_R = C_GATE

V_QN = 0
V_KVN = 256
V_QG = 384
V_KG = 512
V_GGQ = 640
V_GGK = 768

BF16 = jnp.bfloat16
F32 = jnp.float32
NT_DIMS = (((1,), (1,)), ((), ()))


def _dot(a, b):
    return jnp.dot(a, b, preferred_element_type=F32)


def _sigmoid(v):
    return 0.5 * jnp.tanh(0.5 * v) + 0.5


def _split_hi_lo(a):
    hi = a.astype(BF16)
    lo = (a - hi.astype(F32)).astype(BF16)
    return hi, lo


def _params(*sem):
    return pltpu.CompilerParams(dimension_semantics=sem, vmem_limit_bytes=VMEM_LIMIT)


def _const_spec(shape):
    nd = len(shape)
    return pl.BlockSpec(shape, lambda *_: (0,) * nd)


def _mod_kernel(c_ref, w_ref, b_ref, o_ref):
    c = c_ref[...]
    s = c * _sigmoid(c)
    sh, sl = _split_hi_lo(s)
    wh, wl = _split_hi_lo(w_ref[0])
    o_ref[0] = _dot(sh, wh) + _dot(sh, wl) + _dot(sl, wh) + b_ref[0]


def _modulation(c_all, w_mod, b_mod):
    depth, d, d6 = w_mod.shape
    m = c_all.shape[0]
    tn = 1536
    return pl.pallas_call(
        _mod_kernel,
        grid=(depth, d6 // tn),
        in_specs=[pl.BlockSpec((m, d), lambda l, j: (0, 0)),
                  pl.BlockSpec((1, d, tn), lambda l, j: (l, 0, j)),
                  pl.BlockSpec((1, 1, tn), lambda l, j: (l, 0, j))],
        out_specs=pl.BlockSpec((1, m, tn), lambda l, j: (l, 0, j)),
        out_shape=jax.ShapeDtypeStruct((depth, m, d6), F32),
        compiler_params=_params("parallel", "parallel"),
        name="modulation",
    )(c_all, w_mod, b_mod.reshape(depth, 1, d6))


def _rope(n, cos, sa, sb, quarter):
    return n * cos + pltpu.roll(n, LANES - quarter, 1) * sa + pltpu.roll(n, quarter, 1) * sb


def _pre_mixer_kernel(x_ref, sc_ref, sh_ref, vec_ref, w_ref, wuq_ref, wukv_ref, *rest, use_rope, d_model, t_sub):
    for r0 in range(0, x_ref.shape[1], t_sub):
        _pre_mixer_rows(slice(r0, r0 + t_sub), x_ref, sc_ref, sh_ref, vec_ref, w_ref, wuq_ref, wukv_ref, rest,
                        use_rope, d_model)


def _pre_mixer_rows(rows, x_ref, sc_ref, sh_ref, vec_ref, w_ref, wuq_ref, wukv_ref, rest, use_rope, d_model):
    if use_rope:
        mc_ref, msa_ref, msb_ref, gc_ref, gsa_ref, gsb_ref = rest[:6]
        rest = rest[6:]
    qm_ref, km_ref, vm_ref, qg_ref, kg_ref, vg_ref, y_ref, gate_ref = rest

    x = x_ref[0, rows, :]
    inv = lax.rsqrt(jnp.mean(x * x, axis=-1, keepdims=True) + EPS)
    h = (x * inv * vec_ref[0:1, :]) * (1.0 + sc_ref[0]) + sh_ref[0]
    hb = h.astype(BF16)

    lane = lax.broadcasted_iota(jnp.int32, (1, LANES), 1)
    lo = lane < HALF

    def proj(c0, width):
        return _dot(hb, w_ref[:, c0:c0 + width])

    def rms(v, g):
        return v * lax.rsqrt(jnp.mean(v * v, axis=-1, keepdims=True) + EPS) * g

    def seg_inv(xs, n_lo, n_hi):
        sq = xs * xs
        if n_hi is None:
            return lax.rsqrt(jnp.sum(sq, axis=-1, keepdims=True) * (1.0 / n_lo) + EPS)
        s_lo = jnp.sum(jnp.where(lo, sq, 0.0), axis=-1, keepdims=True)
        s_hi = jnp.sum(jnp.where(lo, 0.0, sq), axis=-1, keepdims=True)
        return jnp.where(lo, lax.rsqrt(s_lo * (1.0 / n_lo) + EPS), lax.rsqrt(s_hi * (1.0 / n_hi) + EPS))

    def mla_rope(n):
        return _rope(n, mc_ref[rows, :], msa_ref[rows, :], msb_ref[rows, :], MLA_ROPE // 4) if use_rope else n

    def gqa_rope(n):
        return _rope(n, gc_ref[rows, :], gsa_ref[rows, :], gsb_ref[rows, :], GQA_HD // 4) if use_rope else n

    small = proj(0, C_CA)

    def narrow(c0, width):
        return small[:, c0:c0 + width]

    cqn = rms(narrow(C_CQ, MLA_Q_RANK), vec_ref[1:2, V_QN:V_QN + MLA_Q_RANK])
    q = _dot(cqn.astype(BF16), wuq_ref[...])
    qgain = vec_ref[1:2, V_QG:V_QG + LANES]
    for hh in range(MLA_HEADS):
        xs = q[:, hh * LANES:(hh + 1) * LANES]
        n = mla_rope(xs * seg_inv(xs, MLA_NOPE, MLA_ROPE) * qgain)
        qm_ref[0, hh, rows, :] = (n * (MLA_QK ** -0.5 * LOG2E)).astype(BF16)

    kgain = vec_ref[1:2, V_KG:V_KG + LANES]
    kpe = narrow(C_KPE, LANES)
    kpe = mla_rope(kpe * seg_inv(kpe, MLA_ROPE, None) * kgain)
    ckvn = rms(narrow(C_CKV, MLA_KV_RANK), vec_ref[1:2, V_KVN:V_KVN + MLA_KV_RANK])
    kv = _dot(ckvn.astype(BF16), wukv_ref[...])
    for hh in range(MLA_HEADS):
        xs = kv[:, hh * LANES:(hh + 1) * LANES]
        km_ref[0, hh, rows, :] = (xs * seg_inv(xs, MLA_NOPE, None) * kgain + kpe).astype(BF16)
    ones_slab = jnp.ones((x.shape[0], LANES), BF16)
    v0 = MLA_HEADS * LANES
    for p in range(MLA_HEADS // 2):
        pair = kv[:, v0 + p * LANES:v0 + (p + 1) * LANES].astype(BF16)
        vm_ref[0, 2 * p, rows, :] = jnp.where(lo, pair, ones_slab)
        vm_ref[0, 2 * p + 1, rows, :] = jnp.where(lo, ones_slab, pair)

    gk = narrow(C_GK, LANES)
    ggk = vec_ref[1:2, V_GGK:V_GGK + LANES]
    kg_ref[0, rows, :] = gqa_rope(gk * seg_inv(gk, GQA_HD, GQA_HD) * ggk).astype(BF16)
    gv = narrow(C_GV, LANES).astype(BF16)
    vg_ref[0, 0, rows, :] = jnp.where(lo, gv, ones_slab)
    vg_ref[0, 1, rows, :] = jnp.where(lo, ones_slab, gv)
    ggq = vec_ref[1:2, V_GGQ:V_GGQ + LANES]
    for p in range(GQA_HEADS // 2):
        xs = narrow(C_GQ + p * LANES, LANES)
        n = gqa_rope(xs * seg_inv(xs, GQA_HD, GQA_HD) * ggq)
        qg_ref[0, p, rows, :] = (n * (GQA_HD ** -0.5 * LOG2E)).astype(BF16)

    y_ref[0, rows, :] = proj(C_CA, CONV_CH) * _sigmoid(proj(C_CG, CONV_CH))
    for j in range(N_BRANCH):
        gate_ref[0, rows, j * d_model:(j + 1) * d_model] = _sigmoid(
            proj(C_GATE + j * d_model, d_model)).astype(BF16)


PRE_SUB = 512


def _pre_mixer(x, sc, sh, vec, w_in_r, w_uq_r, w_ukv_r, tables, tm):
    b, n, d = x.shape
    use_rope = tables is not None
    nt = n // tm
    bmap = lambda i, j: (j, 0, 0)
    in_specs = [pl.BlockSpec((1, tm, d), lambda i, j: (j, i, 0)),
                pl.BlockSpec((1, 1, d), bmap),
                pl.BlockSpec((1, 1, d), bmap),
                _const_spec(vec.shape), _const_spec(w_in_r.shape), _const_spec(w_uq_r.shape),
                _const_spec(w_ukv_r.shape)]
    args = [x, sc, sh, vec, w_in_r, w_uq_r, w_ukv_r]
    if use_rope:
        in_specs += [pl.BlockSpec((tm, LANES), lambda i, j: (i, 0))] * 6
        args += list(tables)
    hspec = lambda nh, w=LANES: pl.BlockSpec((1, nh, tm, w), lambda i, j: (j, 0, i, 0))
    tspec = lambda w: pl.BlockSpec((1, tm, w), lambda i, j: (j, i, 0))
    out_specs = [hspec(MLA_HEADS), hspec(MLA_HEADS), hspec(MLA_HEADS), hspec(GQA_HEADS // 2),
                 tspec(LANES), hspec(GQA_KV_HEADS), tspec(CONV_CH), tspec(N_BRANCH * d)]
    sds = jax.ShapeDtypeStruct
    out_shape = [sds((b, MLA_HEADS, n, LANES), BF16), sds((b, MLA_HEADS, n, LANES), BF16),
                 sds((b, MLA_HEADS, n, LANES), BF16), sds((b, GQA_HEADS // 2, n, LANES), BF16),
                 sds((b, n, LANES), BF16), sds((b, GQA_KV_HEADS, n, LANES), BF16),
                 sds((b, n, CONV_CH), F32), sds((b, n, N_BRANCH * d), BF16)]
    return pl.pallas_call(
        functools.partial(_pre_mixer_kernel, use_rope=use_rope, d_model=d, t_sub=min(tm, PRE_SUB)),
        grid=(nt, b), in_specs=in_specs, out_specs=out_specs, out_shape=out_shape,
        compiler_params=_params("parallel", "parallel"),
        name="pre_mixer_rope" if use_rope else "pre_mixer",
    )(*args)


def _attn_kernel(q_ref, *refs, n_seg, head_map, pairs, t_sub, t_piece):
    kv_refs = refs[:2 * n_seg]
    o_ref = refs[2 * n_seg]
    lane = lax.broadcasted_iota(jnp.int32, (1, LANES), 1)
    lo = lane < HALF

    def one_head(hh, rows, out_half):
        qidx, q_half, kidx, vidx = head_map[hh]
        sum_lane = HALF if out_half == 0 else 0
        q = q_ref[0, qidx, rows, :]
        if q_half is not None:
            q = jnp.where(lo if q_half == 0 else jnp.logical_not(lo), q, jnp.zeros_like(q))
        ss_all = [lax.dot_general(q, kv_refs[2 * s][0, kidx], NT_DIMS, preferred_element_type=F32)
                  for s in range(n_seg)]
        outs = []
        for p0 in range(0, ss_all[0].shape[0], t_piece):
            ss = [s[p0:p0 + t_piece] for s in ss_all]
            m = functools.reduce(jnp.maximum, [jnp.max(s, axis=-1, keepdims=True) for s in ss])
            ol = functools.reduce(jnp.add, [_dot(jnp.exp2(ss[s] - m).astype(BF16), kv_refs[2 * s + 1][0, vidx])
                                            for s in range(n_seg)])
            outs.append(ol / ol[:, sum_lane:sum_lane + 1])
        return outs[0] if len(outs) == 1 else jnp.concatenate(outs, axis=0)

    for r0 in range(0, q_ref.shape[2], t_sub):
        rows = slice(r0, r0 + t_sub)
        for p, (h_lo, h_hi) in enumerate(pairs):
            o_ref[0, rows, p * LANES:(p + 1) * LANES] = jnp.where(
                lo, one_head(h_lo, rows, 0), one_head(h_hi, rows, 1)).astype(BF16)


MLA_HEAD_MAP = tuple((h, None, h, h) for h in range(MLA_HEADS))
MLA_PAIRS = tuple((2 * p, 2 * p + 1) for p in range(MLA_HEADS // 2))
GQA_HEAD_MAP = tuple((h % (GQA_HEADS // 2), h // (GQA_HEADS // 2), 0, h // (GQA_HEADS // 2))
                     for h in range(GQA_HEADS))
GQA_PAIRS = tuple((a, a + GQA_HEADS // 2) for a in range(GQA_HEADS // 2))


ATTN_SUB = 512
ATTN_PIECE = 256


def _attention(q, kvs, head_map, pairs, tq, name):
    b, nh, n, _ = q.shape
    in_specs = [pl.BlockSpec((1, nh, tq, LANES), lambda i, j: (i, 0, j, 0))]
    args = [q]
    for a in kvs:
        in_specs.append(pl.BlockSpec((1,) + a.shape[1:], lambda i, j: (i, 0, 0, 0)))
        args.append(a)
    width = len(pairs) * LANES
    return pl.pallas_call(
        functools.partial(_attn_kernel, n_seg=len(kvs) // 2, head_map=head_map, pairs=pairs,
                          t_sub=min(tq, ATTN_SUB), t_piece=min(tq, ATTN_PIECE)),
        grid=(b, n // tq), in_specs=in_specs,
        out_specs=pl.BlockSpec((1, tq, width), lambda i, j: (i, j, 0)),
        out_shape=jax.ShapeDtypeStruct((b, n, width), BF16),
        compiler_params=_params("parallel", "parallel"),
        name=name,
    )(*args)


CONV_PAD = 16
CONV_TAIL = 24
CONV_CHUNK = 256


def _conv_kernel(y_ref, cw_ref, vec_ref, z_ref, buf_ref, ph_ref, *, n):
    ch = y_ref.shape[-1]
    buf_ref[0:CONV_PAD, :] = jnp.zeros((CONV_PAD, ch), F32)
    buf_ref[CONV_PAD + n:CONV_PAD + n + CONV_TAIL, :] = jnp.zeros((CONV_TAIL, ch), F32)
    buf_ref[CONV_PAD:CONV_PAD + n, :] = y_ref[0]
    first = CONV_PAD - CONV_W // 2
    rows = ph_ref.shape[1] - CONV_TAIL
    win_rows = rows + CONV_W + 1

    def chunk(c, carry):
        r0 = pl.multiple_of(c * rows, rows)
        win = buf_ref[pl.ds(r0, win_rows), :]
        for s in range(SUBLANES):
            ph_ref[s] = win[s:s + ph_ref.shape[1], :]
        acc = jnp.zeros((rows, ch), F32) + vec_ref[0:1, :]
        for k in range(CONV_W):
            a, s = divmod(first + k, SUBLANES)
            acc = acc + ph_ref[s, a * SUBLANES:a * SUBLANES + rows, :] * cw_ref[k:k + 1, :]
        mu = jnp.mean(acc, axis=-1, keepdims=True)
        cen = acc - mu
        var = jnp.mean(cen * cen, axis=-1, keepdims=True)
        yn = cen * lax.rsqrt(var + EPS) * vec_ref[1:2, :] + vec_ref[2:3, :]
        z_ref[0, pl.ds(r0, rows), :] = (yn * _sigmoid(yn)).astype(BF16)
        return carry

    lax.fori_loop(0, n // rows, chunk, 0)


def _conv(y, cw, cvec):
    b, n, ch = y.shape
    chunk = min(CONV_CHUNK, n)
    assert n % chunk == 0 and chunk % SUBLANES == 0
    return pl.pallas_call(
        functools.partial(_conv_kernel, n=n),
        grid=(b,),
        in_specs=[pl.BlockSpec((1, n, ch), lambda i: (i, 0, 0)), _const_spec(cw.shape), _const_spec(cvec.shape)],
        out_specs=pl.BlockSpec((1, n, ch), lambda i: (i, 0, 0)),
        out_shape=jax.ShapeDtypeStruct((b, n, ch), BF16),
        scratch_shapes=[pltpu.VMEM((CONV_PAD + n + CONV_TAIL, ch), F32),
                        pltpu.VMEM((SUBLANES, chunk + CONV_TAIL, ch), F32)],
        compiler_params=_params("parallel"),
        name="conv",
    )(y, cw, cvec)


def _merge_route_kernel(x_ref, g_ref, om_ref, oc_ref, og_ref, gate_ref, wm_ref, wc_ref, wg_ref, wo_ref,
                        sc_ref, sh_ref, n2_ref, rw_ref, rb_ref, cin_ref,
                        o_ref, hx_ref, bkt_ref, rank_ref, cnt_ref, *, d_model):
    d = d_model
    merged = (gate_ref[0, :, 0:d].astype(F32) * _dot(om_ref[0], wm_ref[...])
              + gate_ref[0, :, d:2 * d].astype(F32) * _dot(oc_ref[0], wc_ref[...])
              + gate_ref[0, :, 2 * d:3 * d].astype(F32) * _dot(og_ref[0], wg_ref[...]))
    x_new = x_ref[0] + g_ref[0] * _dot(merged.astype(BF16), wo_ref[...])
    o_ref[0] = x_new
    _route(x_new, sc_ref, sh_ref, n2_ref, rw_ref, rb_ref, cin_ref, hx_ref, bkt_ref, rank_ref, cnt_ref, d)


def _merge_route(x, g1, o_mla, z, o_gqa, gates, wm, wc, wg, wo, sc2, sh2, n2, rw, rb, cin, tm):
    b, n, d = x.shape
    nt = n // tm
    sds = jax.ShapeDtypeStruct
    tok = lambda w: pl.BlockSpec((1, tm, w), lambda i, j: (i, j, 0))
    per_b = pl.BlockSpec((1, 1, d), lambda i, j: (i, 0, 0))
    ispec = pl.BlockSpec((1, 1, 1, tm), lambda i, j: (i, j, 0, 0))
    return pl.pallas_call(
        functools.partial(_merge_route_kernel, d_model=d),
        grid=(b, nt),
        in_specs=[tok(d), per_b, tok(o_mla.shape[-1]), tok(z.shape[-1]), tok(o_gqa.shape[-1]), tok(N_BRANCH * d),
                  _const_spec(wm.shape), _const_spec(wc.shape), _const_spec(wg.shape), _const_spec(wo.shape),
                  per_b, per_b, _const_spec(n2.shape), _const_spec(rw.shape), _const_spec(rb.shape),
                  _const_spec(cin.shape)],
        out_specs=[tok(d), tok(d + LANES), ispec, ispec, _const_spec((BUCKET_ROWS, LANES))],
        out_shape=[sds((b, n, d), F32), sds((b, n, d + LANES), F32), sds((b, nt, 1, tm), jnp.int32),
                   sds((b, nt, 1, tm), jnp.int32), sds((BUCKET_ROWS, LANES), F32)],
        compiler_params=_params("arbitrary", "arbitrary"),
        name="merge_route",
    )(x, g1, o_mla, z, o_gqa, gates, wm, wc, wg, wo, sc2, sh2, n2, rw, rb, cin)


def _first_argmax4(v):
    m = functools.reduce(jnp.maximum, v)
    idx = jnp.where(v[0] == m, 0, jnp.where(v[1] == m, 1, jnp.where(v[2] == m, 2, 3)))
    return m, idx


def _pick4(idx, v):
    return jnp.where(idx == 0, v[0], jnp.where(idx == 1, v[1], jnp.where(idx == 2, v[2], v[3])))


def _route(x, sc_ref, sh_ref, g_ref, rw_ref, rb_ref, cin_ref, hx_ref, bkt_ref, rank_ref, cnt_ref, d_model):
    tm = x.shape[0]
    first = (pl.program_id(0) == 0) & (pl.program_id(1) == 0)

    @pl.when(first)
    def _():
        cnt_ref[...] = cin_ref[...]

    inv = lax.rsqrt(jnp.mean(x * x, axis=-1, keepdims=True) + EPS)
    h = (x * inv * g_ref[...]) * (1.0 + sc_ref[0]) + sh_ref[0]
    hx_ref[0, :, 0:d_model] = h

    hh, hl = _split_hi_lo(h)
    rwh, rwl = rw_ref[0], rw_ref[1]
    logits = (lax.dot_general(rwh, hh, NT_DIMS, preferred_element_type=F32)
              + lax.dot_general(rwh, hl, NT_DIMS, preferred_element_type=F32)
              + lax.dot_general(rwl, hh, NT_DIMS, preferred_element_type=F32))
    s_all = _sigmoid(logits)
    sel_all = s_all + rb_ref[...]
    s = [s_all[e:e + 1, :] for e in range(N_EXPERTS)]
    sel = [sel_all[e:e + 1, :] for e in range(N_EXPERTS)]

    gscore = []
    for g in range(N_GROUPS):
        a, b, c, d = sel[4 * g:4 * g + 4]
        gscore.append(functools.reduce(jnp.maximum, [a + b, a + c, a + d, b + c, b + d, c + d]))
    _, gi = _first_argmax4(gscore)
    v = [_pick4(gi, [sel[4 * g + j] for g in range(N_GROUPS)]) for j in range(EXPERTS_PER_GROUP)]
    u = [_pick4(gi, [s[4 * g + j] for g in range(N_GROUPS)]) for j in range(EXPERTS_PER_GROUP)]
    _, i1 = _first_argmax4(v)
    _, i2 = _first_argmax4([jnp.where(i1 == j, -jnp.inf, v[j]) for j in range(EXPERTS_PER_GROUP)])
    e_lo = jnp.minimum(i1, i2)
    e_hi = jnp.maximum(i1, i2)
    pair = jnp.where(e_lo == 0, e_hi - 1, jnp.where(e_lo == 1, e_hi + 1, N_PAIRS - 1))
    bucket = gi * N_PAIRS + pair
    u_lo = _pick4(e_lo, u)
    u_hi = _pick4(e_hi, u)
    den = u_lo + u_hi
    wts = jnp.concatenate([u_lo / den, u_hi / den, jnp.zeros((LANES - 2, tm), F32)], axis=0)
    hx_ref[0, :, d_model:d_model + LANES] = wts.T

    onehot = lax.broadcasted_iota(jnp.int32, (BUCKET_ROWS, tm), 0) == bucket
    oh = jnp.where(onehot, 1.0, 0.0)
    before = lax.broadcasted_iota(jnp.int32, (tm, tm), 0) < lax.broadcasted_iota(jnp.int32, (tm, tm), 1)
    cum = _dot(oh.astype(BF16), jnp.where(before, 1.0, 0.0).astype(BF16))
    rank = jnp.sum(oh * (cum + cnt_ref[:, 0:1]), axis=0, keepdims=True)
    bkt_ref[0, 0] = bucket
    rank_ref[0, 0] = rank.astype(jnp.int32)
    cnt_ref[...] = cnt_ref[...] + jnp.sum(oh, axis=1, keepdims=True)


ROW_UNROLL = 4
N_DMA_PRIORITIES = 2


def _token_rows(t, sub):
    return pl.ds(pl.multiple_of(t * sub, sub), sub)


def _issue_rows(tm, row_copy, pos_ref):
    def issue(i, c):
        for q in range(N_DMA_PRIORITIES):
            r = i * N_DMA_PRIORITIES + q
            row_copy(r, pos_ref[0, 0, r]).start(priority=q)
        return c

    lax.fori_loop(0, tm // N_DMA_PRIORITIES, issue, 0, unroll=ROW_UNROLL)


def _drain_rows(tm, row_copy):
    def drain(r, c):
        row_copy(0, 0).wait()
        return c

    lax.fori_loop(0, tm, drain, 0, unroll=SUBLANES)


def _scatter_kernel(hi_ref, lo_ref, src_ref, hs_in_ref, hs_ref, sem):
    del hs_in_ref
    tm = src_ref.shape[0] * SUBLANES

    def row_copy(i, j, hi, lo):
        return pltpu.make_async_copy(src_ref.at[i, pl.ds(j, 1)], hs_ref.at[hi, pl.ds(lo, 1)], sem)

    def issue(i, c):
        for j in range(SUBLANES):
            r = i * SUBLANES + j
            lo = lo_ref[0, 0, r] & (SUBLANES - 1)
            row_copy(i, j, hi_ref[0, 0, r], lo).start(priority=j % N_DMA_PRIORITIES)
        return c

    lax.fori_loop(0, tm // SUBLANES, issue, 0, unroll=2)
    _drain_rows(tm, lambda r, p: row_copy(0, 0, 0, 0))


def _scatter(pos3, src, hs, tm):
    t8, _, w = src.shape
    ispec = pl.BlockSpec((1, 1, tm), lambda i: (i, 0, 0), memory_space=pltpu.SMEM)
    return pl.pallas_call(
        _scatter_kernel,
        grid=(t8 * SUBLANES // tm,),
        in_specs=[ispec, ispec, pl.BlockSpec((tm // SUBLANES, SUBLANES, w), lambda i: (i, 0, 0)),
                  pl.BlockSpec(memory_space=pl.ANY)],
        out_specs=pl.BlockSpec(memory_space=pl.ANY),
        out_shape=jax.ShapeDtypeStruct(hs.shape, hs.dtype),
        scratch_shapes=[pltpu.SemaphoreType.DMA(())],
        input_output_aliases={3: 0},
        compiler_params=_params("arbitrary"),
        name="scatter_rows",
    )(pos3 // SUBLANES, pos3 % SUBLANES, src, hs)


def _gather_res_kernel(cur_ref, nxt_ref, x_ref, g_ref, ys_ref, o_ref, buf_ref, sems):
    step = pl.program_id(0)
    tm = x_ref.shape[1]

    def row_copy_on(slot):
        def row_copy(r, p):
            return pltpu.make_async_copy(ys_ref.at[_token_rows(p, OUT_SUB)],
                                         buf_ref.at[slot, _token_rows(r, OUT_SUB)], sems.at[slot])
        return row_copy

    slot = step % 2

    @pl.when(step == 0)
    def _():
        _issue_rows(tm, row_copy_on(slot), cur_ref)

    @pl.when(step + 1 < pl.num_programs(0))
    def _():
        _issue_rows(tm, row_copy_on(1 - slot), nxt_ref)

    _drain_rows(tm, row_copy_on(slot))
    moe = jnp.concatenate([buf_ref[slot, pl.ds(j, tm, stride=OUT_SUB), :] for j in range(OUT_SUB)], axis=-1)
    o_ref[0] = x_ref[0] + g_ref[0] * moe


def _gather_res(pos3, x, g2, ys, tm):
    b, n, d = x.shape
    nt = n // tm
    steps = b * nt
    cur = pl.BlockSpec((1, 1, tm), lambda i: (i, 0, 0), memory_space=pltpu.SMEM)
    nxt = pl.BlockSpec((1, 1, tm), lambda i: (jnp.minimum(i + 1, steps - 1), 0, 0), memory_space=pltpu.SMEM)
    tok = pl.BlockSpec((1, tm, d), lambda i: (i // nt, i % nt, 0))
    return pl.pallas_call(
        _gather_res_kernel,
        grid=(steps,),
        in_specs=[cur, nxt, tok, pl.BlockSpec((1, 1, d), lambda i: (i // nt, 0, 0)), pl.BlockSpec(memory_space=pl.ANY)],
        out_specs=tok,
        out_shape=jax.ShapeDtypeStruct(x.shape, F32),
        scratch_shapes=[pltpu.VMEM((2, tm * OUT_SUB, LANES), F32), pltpu.SemaphoreType.DMA((2,))],
        compiler_params=_params("arbitrary"),
        name="gather_residual",
    )(pos3, pos3, x, g2, ys)


def _experts_kernel(e1_ref, e2_ref, blk_ref, valid_ref, hs_ref, wg1_ref, wu1_ref, wd1_ref, wg2_ref, wu2_ref,
                    wd2_ref, o_ref, *, d_model):
    valid = valid_ref[pl.program_id(0)] > 0

    @pl.when(jnp.logical_not(valid))
    def _():
        o_ref[...] = jnp.zeros(o_ref.shape, F32)

    @pl.when(valid)
    def _():
        tm = hs_ref.shape[0]
        xb = hs_ref[:, 0:d_model].astype(BF16)

        def half(wg_ref, wu_ref, w):
            gte = _dot(xb, wg_ref[0])
            return (gte * _sigmoid(gte) * _dot(xb, wu_ref[0]) * w).astype(BF16)

        h1 = half(wg1_ref, wu1_ref, hs_ref[:, d_model:d_model + 1])
        h2 = half(wg2_ref, wu2_ref, hs_ref[:, d_model + 1:d_model + 2])
        out = _dot(h1, wd1_ref[0]) + _dot(h2, wd2_ref[0])
        for j in range(d_model // LANES):
            o_ref[pl.ds(j, tm, stride=OUT_SUB), :] = out[:, j * LANES:(j + 1) * LANES]


def _experts(e1, e2, blk, valid, hs, wg, wu, wd, tm):
    tp, w = hs.shape
    d, f = wg.shape[-2:]
    assert d == OUT_SUB * LANES and w == d + LANES
    up1 = pl.BlockSpec((1, d, f), lambda i, e1, e2, blk, valid: (e1[i], 0, 0))
    up2 = pl.BlockSpec((1, d, f), lambda i, e1, e2, blk, valid: (e2[i], 0, 0))
    dn1 = pl.BlockSpec((1, f, d), lambda i, e1, e2, blk, valid: (e1[i], 0, 0))
    dn2 = pl.BlockSpec((1, f, d), lambda i, e1, e2, blk, valid: (e2[i], 0, 0))
    grid_spec = pltpu.PrefetchScalarGridSpec(
        num_scalar_prefetch=4, grid=(tp // tm,),
        in_specs=[pl.BlockSpec((tm, w), lambda i, e1, e2, blk, valid: (blk[i], 0)), up1, up1, dn1, up2, up2, dn2],
        out_specs=pl.BlockSpec((tm * OUT_SUB, LANES), lambda i, e1, e2, blk, valid: (i, 0)))
    return pl.pallas_call(
        functools.partial(_experts_kernel, d_model=d),
        grid_spec=grid_spec,
        out_shape=jax.ShapeDtypeStruct((tp * OUT_SUB, LANES), F32),
        compiler_params=_params("arbitrary"),
        name="moe_experts",
    )(e1, e2, blk, valid, hs, wg, wu, wd, wg, wu, wd)


PAIR_LO = np.array([0, 0, 0, 1, 1, 2], np.int32)
PAIR_HI = np.array([1, 2, 3, 2, 3, 3], np.int32)


def _moe_plan(counts, n_tiles, tm):
    c = counts[:N_BUCKETS]
    nt = (c + tm - 1) // tm
    tile_end = jnp.cumsum(nt)
    row_off = (tile_end - nt) * tm
    n_valid = tile_end[-1]
    i = jnp.arange(n_tiles, dtype=jnp.int32)
    blk = jnp.minimum(i, n_valid - 1)
    bkt = jnp.minimum(jnp.sum(blk[:, None] >= tile_end[None, :], axis=1), N_BUCKETS - 1).astype(jnp.int32)
    grp = bkt // N_PAIRS
    e1 = grp * EXPERTS_PER_GROUP + jnp.asarray(PAIR_LO)[bkt % N_PAIRS]
    e2 = grp * EXPERTS_PER_GROUP + jnp.asarray(PAIR_HI)[bkt % N_PAIRS]
    return row_off, e1, e2, blk.astype(jnp.int32), (i < n_valid).astype(jnp.int32)


def _sorted_rows_buffer(n_tokens, w, tm_exp):
    n_tiles = -(-n_tokens // tm_exp) + N_BUCKETS
    return jnp.zeros((n_tiles * tm_exp // SUBLANES, SUBLANES, w), F32)


def _moe(parts, cnt, hs, wg, wu, wd, tm_exp):
    n_tiles = hs.shape[0] * SUBLANES // tm_exp
    w = hs.shape[-1]
    row_off, e1, e2, blk, valid = _moe_plan(cnt[:, 0].astype(jnp.int32), n_tiles, tm_exp)
    addr = []
    for x, g2, hx, bkt, rank, tm in parts:
        start = sum(jnp.where(bkt == k, row_off[k], 0) for k in range(N_BUCKETS))
        pos = (start + rank).reshape(-1, 1, tm)
        addr.append(pos)
        hs = _scatter(pos, hx.reshape(-1, SUBLANES, w), hs, tm)
    ys = _experts(e1, e2, blk, valid, hs.reshape(-1, w), wg, wu, wd, tm_exp)
    outs = [_gather_res(pos, x, g2, ys, tm) for (x, g2, hx, bkt, rank, tm), pos in zip(parts, addr)]
    return outs, hs


def _relayout_w_in(w_in, d_model):
    sizes = (MLA_Q_RANK, MLA_KV_RANK, MLA_ROPE, 2 * CONV_CH, GQA_HEADS * GQA_HD, GQA_KV_HEADS * GQA_HD,
             GQA_KV_HEADS * GQA_HD, N_BRANCH * d_model)
    offs = np.cumsum((0,) + sizes)
    cq, ckv, kpe, conv, gq, gk, gv, gates = [w_in[..., offs[i]:offs[i + 1]] for i in range(len(sizes))]
    lead = w_in.shape[:-1]
    z = lambda n: jnp.zeros(lead + (n,), w_in.dtype)
    head = lambda hh: gq[..., hh * GQA_HD:(hh + 1) * GQA_HD]
    gq_slabs = [head(hh) for p in range(GQA_HEADS // 2) for hh in (p, p + GQA_HEADS // 2)]
    out = jnp.concatenate([cq, ckv, z(HALF), kpe, z(HALF - MLA_ROPE), gk, gv] + gq_slabs + [conv, gates], axis=-1)
    return out.astype(BF16)


def _relayout_uq(w_uq):
    l, r, _ = w_uq.shape
    w = w_uq.reshape(l, r, MLA_HEADS, MLA_QK)
    w = jnp.pad(w, ((0, 0), (0, 0), (0, 0), (0, LANES - MLA_QK)))
    return w.reshape(l, r, MLA_HEADS * LANES).astype(BF16)


def _relayout_ukv(w_ukv):
    l, r, _ = w_ukv.shape
    w = w_ukv.reshape(l, r, MLA_HEADS, MLA_NOPE + MLA_V)
    k = jnp.pad(w[..., :MLA_NOPE], ((0, 0), (0, 0), (0, 0), (0, LANES - MLA_NOPE)))
    v = w[..., MLA_NOPE:]
    return jnp.concatenate([k.reshape(l, r, MLA_HEADS * LANES), v.reshape(l, r, MLA_HEADS * MLA_V)],
                           axis=-1).astype(BF16)


def _rope_tables(n, r):
    rows = n // GRID_W
    row = jnp.repeat(jnp.arange(rows, dtype=jnp.int32), GRID_W).astype(F32)
    col = jnp.tile(jnp.arange(GRID_W, dtype=jnp.int32), rows).astype(F32)
    dim = r // 2
    inv = ROPE_BASE ** (-jnp.arange(0, dim, 2, dtype=F32) / dim)
    ar, ac = row[:, None] * inv[None, :], col[:, None] * inv[None, :]
    zero = jnp.zeros_like(ar)
    cos = jnp.concatenate([jnp.cos(ar), jnp.cos(ar), jnp.cos(ac), jnp.cos(ac)], axis=-1)
    sa = jnp.concatenate([-jnp.sin(ar), zero, -jnp.sin(ac), zero], axis=-1)
    sb = jnp.concatenate([zero, jnp.sin(ar), zero, jnp.sin(ac)], axis=-1)
    return cos, sa, sb


def _all_tables(n):
    mc, msa, msb = _rope_tables(n, MLA_ROPE)
    one = jnp.ones((n, MLA_NOPE), F32)
    zero = jnp.zeros((n, MLA_NOPE), F32)
    pad1 = jnp.ones((n, LANES - MLA_QK), F32)
    pad0 = jnp.zeros((n, LANES - MLA_QK), F32)
    mla = (jnp.concatenate([one, mc, pad1], -1), jnp.concatenate([zero, msa, pad0], -1),
           jnp.concatenate([zero, msb, pad0], -1))
    gqa = tuple(jnp.concatenate([t, t], -1) for t in _rope_tables(n, GQA_HD))
    return mla + gqa


def kernel(x, c, ctx, c_ctx, w_mod, b_mod, norm1_g, norm2_g, w_in, mla_q_norm, mla_kv_norm, mla_w_uq, mla_w_ukv,
           mla_q_gain, mla_k_gain, mla_w_o, conv_w, conv_b, conv_ln_g, conv_ln_b, conv_w_o, gqa_q_gain, gqa_k_gain,
           gqa_w_o, w_out, router_w, router_bias, moe_w_gate, moe_w_up, moe_w_down):
    b, n, d = x.shape
    n_ctx = ctx.shape[1]
    depth = w_mod.shape[0]
    assert n % GRID_W == 0 and n % 512 == 0 and n_ctx % LANES == 0 and d % LANES == 0

    tm = 512
    tm_ctx = min(n_ctx, 256)
    tq = 512
    tm_exp = 512

    m_rows = -(-(b + 1) // 8) * 8
    c_all = jnp.concatenate([c, c_ctx[None, :], jnp.zeros((m_rows - b - 1, d), F32)], axis=0)
    mod = _modulation(c_all, w_mod, b_mod)

    w_in_r = _relayout_w_in(w_in, d)
    w_uq_r = _relayout_uq(mla_w_uq)
    w_ukv_r = _relayout_ukv(mla_w_ukv)
    zl = lambda k: jnp.zeros((depth, k), F32)
    qg128 = jnp.concatenate([mla_q_gain, zl(LANES - MLA_QK)], -1)
    kg128 = jnp.concatenate([mla_k_gain, zl(LANES - MLA_QK)], -1)
    row1 = jnp.concatenate([mla_q_norm, mla_kv_norm, qg128, kg128, gqa_q_gain, gqa_q_gain, gqa_k_gain, gqa_k_gain,
                            zl(d - V_GGK - LANES)], -1)
    vec = jnp.stack([norm1_g, row1], axis=1)
    gqa_perm = np.concatenate([np.arange(GQA_HD) + GQA_HD * hh for pair in GQA_PAIRS for hh in pair])
    wm_o = mla_w_o.astype(BF16)
    wg_o = gqa_w_o[:, gqa_perm, :].astype(BF16)
    wc_o = conv_w_o.astype(BF16)
    wo = w_out.astype(BF16)
    cw = jnp.pad(conv_w, ((0, 0), (0, 32 - CONV_W), (0, 0)))
    cvec = jnp.pad(jnp.stack([conv_b, conv_ln_g, conv_ln_b], axis=1), ((0, 0), (0, 5), (0, 0)))
    rwt = router_w.T
    rw_hi = rwt.astype(BF16)
    rw = jnp.stack([rw_hi, (rwt - rw_hi.astype(F32)).astype(BF16)])
    rb = router_bias.reshape(N_EXPERTS, 1)
    wg_e, wu_e, wd_e = moe_w_gate.astype(BF16), moe_w_up.astype(BF16), moe_w_down.astype(BF16)
    tables = _all_tables(n)

    xc = ctx
    assert d == OUT_SUB * LANES
    hs_buf = _sorted_rows_buffer(b * (n + n_ctx), d + LANES, tm_exp)
    for l in range(depth):
        last = l == depth - 1
        sh1, sc1, g1, sh2, sc2, g2 = [t[:, None, :] for t in jnp.split(mod[l, :b], 6, axis=-1)]
        csh1, csc1, cg1, csh2, csc2, cg2 = [jnp.broadcast_to(t[None, :, :], (b, 1, d))
                                            for t in jnp.split(mod[l, b:b + 1], 6, axis=-1)]
        qm, km, vm, qg, kg, vg, y, gates = _pre_mixer(x, sc1, sh1, vec[l], w_in_r[l], w_uq_r[l], w_ukv_r[l],
                                                      tables, tm)
        cqm, ckm, cvm, cqg, ckg, cvg, cy, cgates = _pre_mixer(xc, csc1, csh1, vec[l], w_in_r[l], w_uq_r[l],
                                                              w_ukv_r[l], None, tm_ctx)
        kg4, vg4, ckg4, cvg4 = kg[:, None], vg, ckg[:, None], cvg
        o_mla = _attention(qm, [km, vm, ckm, cvm], MLA_HEAD_MAP, MLA_PAIRS, tq, "attn_mla")
        o_gqa = _attention(qg, [kg4, vg4, ckg4, cvg4], GQA_HEAD_MAP, GQA_PAIRS, tq, "attn_gqa")
        z = _conv(y, cw[l], cvec[l])
        n2 = norm2_g[l][None, :]
        out_w = (wm_o[l], wc_o[l], wg_o[l], wo[l])
        x, hx, bkt, rank, cnt = _merge_route(x, g1, o_mla, z, o_gqa, gates, *out_w, sc2, sh2, n2, rw, rb,
                                             jnp.zeros((BUCKET_ROWS, LANES), F32), tm)
        moe_parts = [(x, g2, hx, bkt, rank, tm)]
        if not last:
            co_mla = _attention(cqm, [ckm, cvm], MLA_HEAD_MAP, MLA_PAIRS, tm_ctx, "attn_mla_ctx")
            co_gqa = _attention(cqg, [ckg4, cvg4], GQA_HEAD_MAP, GQA_PAIRS, tm_ctx, "attn_gqa_ctx")
            cz = _conv(cy, cw[l], cvec[l])
            xc, chx, cbkt, crank, cnt = _merge_route(xc, cg1, co_mla, cz, co_gqa, cgates, *out_w, csc2, csh2, n2,
                                                     rw, rb, cnt, tm_ctx)
            moe_parts.append((xc, cg2, chx, cbkt, crank, tm_ctx))
        outs, hs_buf = _moe(moe_parts, cnt, hs_buf, wg_e[l], wu_e[l], wd_e[l], tm_exp)
        x = outs[0]
        if not last:
            xc = outs[1]
    return x
```
